```python
import math
import jax, jax.numpy as jnp
from jax import lax
import numpy as np

D_MODEL = 1024
BATCH = 4
SEQ = 8192
DEPTH = 4

QBLK = 128
SB_HEADS = 8
SB_HEAD_DIM = 64
SB_WIDTH = SB_HEADS * SB_HEAD_DIM
SSM_HEADS = 16
SSM_HEAD_DIM = 64
SSM_INNER = SSM_HEADS * SSM_HEAD_DIM
SSM_GROUPS = 4
SSM_STATE = 128
SSM_CONV = 4
SSM_CHUNK = 128
SSM_CONV_DIM = SSM_INNER + 2 * SSM_GROUPS * SSM_STATE
DT_MIN = 1e-3
DT_MAX = 0.1
NSA_HEADS = 16
NSA_KV_GROUPS = 4
NSA_HEAD_DIM = 64
NSA_Q = NSA_HEADS * NSA_HEAD_DIM
NSA_KV = NSA_KV_GROUPS * NSA_HEAD_DIM
CMP_BLOCK = 32
CMP_STRIDE = 16
CMP_HIDDEN = 256
SEL_BLOCK = 64
N_SEL = 16
WINDOW = 512
N_BRANCH = 3
REL_BUCKETS = 32
REL_MAX_EXACT = 16
REL_MAX_DIST = 1024
D_FF = 2816
FFN_CONV = 3

EPS = 1e-6
NEG = -1e30
FORCE = 1e4
N_EVEN = (DEPTH + 1) // 2
N_ODD = DEPTH // 2
AB_IN = 3 * SB_WIDTH + SSM_INNER + SSM_CONV_DIM + SSM_HEADS
AB_MIX = SB_WIDTH + SSM_INNER
NSA_IN = NSA_Q + 6 * NSA_KV + NSA_HEADS * N_BRANCH

kernel_name = "hybrid_sb_ssd_nsa_convffn_trunk"


def rmsnorm(x, g):
    xf = x.astype(jnp.float32)
    y = xf * lax.rsqrt(jnp.mean(xf * xf, axis=-1, keepdims=True) + EPS)
    return (y * g.astype(jnp.float32)).astype(x.dtype)


def causal_dwconv(x, w, b):
    k_len, ch = w.shape
    y = lax.conv_general_dilated(x, w[:, None, :].astype(x.dtype), window_strides=(1,),
                                 padding=[(k_len - 1, 0)],
                                 dimension_numbers=("NWC", "WIO", "NWC"),
                                 feature_group_count=ch)
    return y + b.astype(x.dtype)


def rel_bucket(n):
    n = jnp.maximum(n, 0)
    nf = jnp.maximum(n, 1).astype(jnp.float32)
    large = REL_MAX_EXACT + (jnp.log(nf / REL_MAX_EXACT) / math.log(REL_MAX_DIST / REL_MAX_EXACT)
                             * (REL_BUCKETS - REL_MAX_EXACT)).astype(jnp.int32)
    large = jnp.minimum(large, REL_BUCKETS - 1)
    return jnp.where(n < REL_MAX_EXACT, n, large)


def masked_softmax(logits, mask):
    logits = jnp.where(mask, logits.astype(jnp.float32), NEG)
    return jnp.where(mask, jax.nn.softmax(logits, axis=-1), 0.0)


def stick_breaking(q, k, v):
    b, h, s, d = q.shape
    scale = d ** -0.5
    kpos = jnp.arange(s)

    def block(i):
        q0 = i * QBLK
        qb = lax.dynamic_slice_in_dim(q, q0, QBLK, axis=2)
        z = jnp.einsum("bhqd,bhkd->bhqk", qb, k).astype(jnp.float32) * scale
        t = q0 + jnp.arange(QBLK)
        mask = kpos[None, :] < t[:, None]
        log_rest = jnp.where(mask, jax.nn.log_sigmoid(-z), 0.0)
        suffix = lax.cumsum(log_rest, axis=3, reverse=True) - log_rest
        w = jnp.where(mask, jnp.exp(jax.nn.log_sigmoid(z) + suffix), 0.0)
        return jnp.einsum("bhqk,bhkd->bhqd", w.astype(v.dtype), v)

    o = lax.map(block, jnp.arange(s // QBLK))
    return o.transpose(1, 2, 0, 3, 4).reshape(b, h, s, d)


def ssd(x, dt, a_neg, bm, cm):
    b, s, h, p = x.shape
    g, n = bm.shape[2], bm.shape[3]
    hg = h // g
    ln = SSM_CHUNK
    nc = s // ln
    a = (dt * a_neg).reshape(b, nc, ln, g, hg)
    xdt = (x.astype(jnp.float32) * dt[..., None]).reshape(b, nc, ln, g, hg, p)
    bc = bm.astype(jnp.float32).reshape(b, nc, ln, g, n)
    cc = cm.astype(jnp.float32).reshape(b, nc, ln, g, n)
    a_cs = jnp.cumsum(a, axis=2)
    causal = jnp.tril(jnp.ones((ln, ln), dtype=bool))
    seg = a_cs[:, :, :, None] - a_cs[:, :, None]
    decay = jnp.exp(jnp.where(causal[None, None, :, :, None, None], seg, NEG))
    cb = jnp.einsum("bclgn,bcsgn->bclsg", cc, bc)
    y_diag = jnp.einsum("bclsg,bclsgh,bcsghp->bclghp", cb, decay, xdt)
    decay_out = jnp.exp(a_cs[:, :, -1:] - a_cs)
    states = jnp.einsum("bcsgn,bcsgh,bcsghp->bcghpn", bc, decay_out, xdt)
    chunk_decay = jnp.exp(a_cs[:, :, -1])

    def step(h_prev, inp):
        dec, st = inp
        return dec[..., None, None] * h_prev + st, h_prev

    h0 = jnp.zeros((b, g, hg, p, n), jnp.float32)
    _, prev = lax.scan(step, h0, (chunk_decay.transpose(1, 0, 2, 3), states.transpose(1, 0, 2, 3, 4, 5)))
    prev = prev.transpose(1, 0, 2, 3, 4, 5)
    y_off = jnp.einsum("bclgn,bcghpn,bclgh->bclghp", cc, prev, jnp.exp(a_cs))
    return (y_diag + y_off).reshape(b, s, h, p)


def mixer_ab(h, w_in, conv_w, conv_b, dt_bias, a_log, d_skip, ssm_norm, w_out):
    b, s, _ = h.shape
    proj = h @ w_in
    o3 = 3 * SB_WIDTH
    q, k, v, z, xbc, dt_raw = jnp.split(
        proj, [SB_WIDTH, 2 * SB_WIDTH, o3, o3 + SSM_INNER, o3 + SSM_INNER + SSM_CONV_DIM], axis=-1)
    heads = lambda t: t.reshape(b, s, SB_HEADS, SB_HEAD_DIM).transpose(0, 2, 1, 3)
    o_sb = stick_breaking(heads(q), heads(k), heads(v)).transpose(0, 2, 1, 3).reshape(b, s, SB_WIDTH)
    xbc = jax.nn.silu(causal_dwconv(xbc, conv_w, conv_b))
    xs, bm, cm = jnp.split(xbc, [SSM_INNER, SSM_INNER + SSM_GROUPS * SSM_STATE], axis=-1)
    dt = jax.nn.softplus(dt_raw.astype(jnp.float32) + dt_bias.astype(jnp.float32))
    a_neg = -jnp.exp(a_log.astype(jnp.float32))
    xs_h = xs.reshape(b, s, SSM_HEADS, SSM_HEAD_DIM)
    y = ssd(xs_h, dt, a_neg, bm.reshape(b, s, SSM_GROUPS, SSM_STATE), cm.reshape(b, s, SSM_GROUPS, SSM_STATE))
    y = y + xs_h.astype(jnp.float32) * d_skip.astype(jnp.float32)[:, None]
    y = y.reshape(b, s, SSM_INNER)
    y = rmsnorm(y * jax.nn.silu(z.astype(jnp.float32)), ssm_norm).astype(h.dtype)
    return jnp.concatenate([o_sb.astype(h.dtype), y], axis=-1) @ w_out


def compress(kv, pe, w1, b1, w2):
    b, s, g, d = kv.shape
    halves = kv.reshape(b, s // CMP_STRIDE, CMP_STRIDE, g, d)
    blocks = jnp.concatenate([halves[:, :-1], halves[:, 1:]], axis=2)
    blocks = blocks + pe[None, None, :, None, :]
    flat = blocks.transpose(0, 1, 3, 2, 4).reshape(b, s // CMP_STRIDE - 1, g, CMP_BLOCK * d)
    return jax.nn.silu(flat @ w1 + b1) @ w2


def mixer_nsa(h, rel_bias, w_in, cmp_w1, cmp_b1, cmp_w2, cmp_pe, w_out):
    b, s, _ = h.shape
    g, hg, d = NSA_KV_GROUPS, NSA_HEADS // NSA_KV_GROUPS, NSA_HEAD_DIM
    proj = h @ w_in
    parts = jnp.split(proj, [NSA_Q + i * NSA_KV for i in range(7)], axis=-1)
    q = parts[0].reshape(b, s, g, hg, d)
    kc, vc, ksl, vsl, kw, vw = [t.reshape(b, s, g, d) for t in parts[1:7]]
    gates = jax.nn.sigmoid(parts[7].astype(jnp.float32)).reshape(b, s, g, hg, N_BRANCH)
    kcmp = compress(kc, cmp_pe[0], cmp_w1[0], cmp_b1[0], cmp_w2[0])
    vcmp = compress(vc, cmp_pe[1], cmp_w1[1], cmp_b1[1], cmp_w2[1])
    nc = kcmp.shape[1]
    ns = s // SEL_BLOCK
    n_sel = min(N_SEL, ns)
    cmp_end = CMP_STRIDE * jnp.arange(nc) + CMP_BLOCK - 1
    ci = (jnp.arange(nc) * CMP_STRIDE)[:, None]
    sj = (jnp.arange(ns) * SEL_BLOCK)[None, :]
    overlap = ((ci < sj + SEL_BLOCK) & (ci + CMP_BLOCK > sj)).astype(jnp.float32)
    ks_blk = ksl.reshape(b, ns, SEL_BLOCK, g, d).transpose(0, 3, 1, 2, 4)
    vs_blk = vsl.reshape(b, ns, SEL_BLOCK, g, d).transpose(0, 3, 1, 2, 4)
    kw_pad = jnp.pad(kw, ((0, 0), (WINDOW, 0), (0, 0), (0, 0)))
    vw_pad = jnp.pad(vw, ((0, 0), (WINDOW, 0), (0, 0), (0, 0)))
    bias_gh = rel_bias.reshape(g, hg, REL_BUCKETS)
    bias_t = bias_gh.transpose(0, 2, 1)
    b_ix = jnp.arange(b)[:, None, None, None]
    g_ix = jnp.arange(g)[None, :, None, None]
    jj = jnp.arange(ns)
    scale = d ** -0.5

    def block(i):
        q0 = i * QBLK
        t = q0 + jnp.arange(QBLK)
        qb = lax.dynamic_slice_in_dim(q, q0, QBLK, axis=1).transpose(0, 2, 3, 1, 4)
        lc = (jnp.einsum("bghqd,bngd->bghqn", qb, kcmp).astype(jnp.float32) * scale
              + bias_gh[:, :, rel_bucket(t[:, None] - cmp_end[None])])
        pc = masked_softmax(lc, cmp_end[None] <= t[:, None])
        oc = jnp.einsum("bghqn,bngd->bghqd", pc.astype(vcmp.dtype), vcmp)
        imp = jnp.einsum("bghqn,nj->bgqj", pc, overlap)
        cur = t // SEL_BLOCK
        valid = (jj[None] * SEL_BLOCK) <= t[:, None]
        forced = (jj[None] == 0) | (jj[None] == cur[:, None]) | (jj[None] == cur[:, None] - 1)
        imp = jnp.where(forced, FORCE, jnp.where(valid, imp, -1.0))
        _, idx = lax.top_k(imp, n_sel)
        ksel = ks_blk[b_ix, g_ix, idx].reshape(b, g, QBLK, n_sel * SEL_BLOCK, d)
        vsel = vs_blk[b_ix, g_ix, idx].reshape(b, g, QBLK, n_sel * SEL_BLOCK, d)
        pos = (idx[..., None] * SEL_BLOCK + jnp.arange(SEL_BLOCK)).reshape(b, g, QBLK, n_sel * SEL_BLOCK)
        rel = t[:, None] - pos
        bias_s = jnp.moveaxis(bias_t[g_ix, rel_bucket(rel)], -1, 2)
        ls = jnp.einsum("bghqd,bgqkd->bghqk", qb, ksel).astype(jnp.float32) * scale + bias_s
        ps = masked_softmax(ls, (rel >= 0)[:, :, None])
        osl = jnp.einsum("bghqk,bgqkd->bghqd", ps.astype(vsel.dtype), vsel)
        kwb = lax.dynamic_slice_in_dim(kw_pad, q0, QBLK + WINDOW, axis=1)
        vwb = lax.dynamic_slice_in_dim(vw_pad, q0, QBLK + WINDOW, axis=1)
        spos = q0 - WINDOW + jnp.arange(QBLK + WINDOW)
        relw = t[:, None] - spos[None]
        lw = (jnp.einsum("bghqd,bkgd->bghqk", qb, kwb).astype(jnp.float32) * scale
              + bias_gh[:, :, rel_bucket(relw)])
        pw = masked_softmax(lw, (relw >= 0) & (relw < WINDOW) & (spos[None] >= 0))
        ow = jnp.einsum("bghqk,bkgd->bghqd", pw.astype(vwb.dtype), vwb)
        gb = lax.dynamic_slice_in_dim(gates, q0, QBLK, axis=1).transpose(0, 2, 3, 1, 4)
        o = gb[..., 0:1] * oc + gb[..., 1:2] * osl + gb[..., 2:3] * ow
        return o.transpose(0, 3, 1, 2, 4).reshape(b, QBLK, NSA_Q).astype(h.dtype)

    o = lax.map(block, jnp.arange(s // QBLK))
    o = o.transpose(1, 0, 2, 3).reshape(b, s, NSA_Q)
    return o @ w_out


def conv_ffn(h, w_up, conv_w, conv_b, w_down):
    u = causal_dwconv(h @ w_up, conv_w, conv_b)
    gate, val = jnp.split(u, 2, axis=-1)
    return (jax.nn.gelu(gate, approximate=True) * val) @ w_down


def setup_inputs(seed: int = 0) -> dict:
    key = jax.random.key(seed)
    ks = jax.random.split(key, 24)

    def nrm(k, shape, scale):
        return jax.random.normal(k, shape, jnp.float32) * scale

    dt0 = jnp.exp(jax.random.uniform(ks[9], (N_EVEN, SSM_HEADS), jnp.float32)
                  * (math.log(DT_MAX) - math.log(DT_MIN)) + math.log(DT_MIN))
    return {
        "x": nrm(ks[0], (BATCH, SEQ, D_MODEL), 1.0),
        "c": nrm(ks[1], (BATCH, D_MODEL), 1.0),
        "rel_bias": nrm(ks[2], (NSA_HEADS, REL_BUCKETS), 0.5),
        "ada_w": nrm(ks[3], (DEPTH, D_MODEL, 6 * D_MODEL), D_MODEL ** -0.5),
        "ada_b": nrm(ks[4], (DEPTH, 6 * D_MODEL), 0.01),
        "norm_g": 1.0 + nrm(ks[5], (DEPTH, 4, D_MODEL), 0.05),
        "ab_w_in": nrm(ks[6], (N_EVEN, D_MODEL, AB_IN), D_MODEL ** -0.5),
        "ab_conv_w": nrm(ks[7], (N_EVEN, SSM_CONV, SSM_CONV_DIM), SSM_CONV ** -0.5),
        "ab_conv_b": nrm(ks[8], (N_EVEN, SSM_CONV_DIM), 0.01),
        "ab_dt_bias": dt0 + jnp.log(-jnp.expm1(-dt0)),
        "ab_a_log": jnp.log(jax.random.uniform(ks[10], (N_EVEN, SSM_HEADS), jnp.float32, 1.0, 16.0)),
        "ab_d_skip": 1.0 + nrm(ks[11], (N_EVEN, SSM_HEADS), 0.1),
        "ab_ssm_norm": 1.0 + nrm(ks[12], (N_EVEN, SSM_INNER), 0.05),
        "ab_w_out": nrm(ks[13], (N_EVEN, AB_MIX, D_MODEL), AB_MIX ** -0.5),
        "nsa_w_in": nrm(ks[14], (N_ODD, D_MODEL, NSA_IN), D_MODEL ** -0.5),
        "nsa_cmp_w1": nrm(ks[15], (N_ODD, 2, CMP_BLOCK * NSA_HEAD_DIM, CMP_HIDDEN), (CMP_BLOCK * NSA_HEAD_DIM) ** -0.5),
        "nsa_cmp_b1": nrm(ks[16], (N_ODD, 2, CMP_HIDDEN), 0.01),
        "nsa_cmp_w2": nrm(ks[17], (N_ODD, 2, CMP_HIDDEN, NSA_HEAD_DIM), CMP_HIDDEN ** -0.5),
        "nsa_cmp_pe": nrm(ks[18], (N_ODD, 2, CMP_BLOCK, NSA_HEAD_DIM), 0.02),
        "nsa_w_out": nrm(ks[19], (N_ODD, NSA_Q, D_MODEL), NSA_Q ** -0.5),
        "ffn_w_up": nrm(ks[20], (DEPTH, D_MODEL, 2 * D_FF), D_MODEL ** -0.5),
        "ffn_conv_w": nrm(ks[21], (DEPTH, FFN_CONV, 2 * D_FF), FFN_CONV ** -0.5),
        "ffn_conv_b": nrm(ks[22], (DEPTH, 2 * D_FF), 0.01),
        "ffn_w_down": nrm(ks[23], (DEPTH, D_FF, D_MODEL), D_FF ** -0.5),
    }


def reference(x, c, rel_bias, ada_w, ada_b, norm_g, ab_w_in, ab_conv_w, ab_conv_b, ab_dt_bias,
              ab_a_log, ab_d_skip, ab_ssm_norm, ab_w_out, nsa_w_in, nsa_cmp_w1, nsa_cmp_b1,
              nsa_cmp_w2, nsa_cmp_pe, nsa_w_out, ffn_w_up, ffn_conv_w, ffn_conv_b, ffn_w_down):
    mod = (jnp.einsum("bd,lde->lbe", jax.nn.silu(c), ada_w) + ada_b[:, None, :])[:, :, None, :]
    for l in range(DEPTH):
        shift1, scale1, gate1, shift2, scale2, gate2 = jnp.split(mod[l], 6, axis=-1)
        h = rmsnorm(x, norm_g[l, 0]) * (1.0 + scale1) + shift1
        if l % 2 == 0:
            e = l // 2
            out = mixer_ab(h, ab_w_in[e], ab_conv_w[e], ab_conv_b[e], ab_dt_bias[e], ab_a_log[e],
                           ab_d_skip[e], ab_ssm_norm[e], ab_w_out[e])
        else:
            o = l // 2
            out = mixer_nsa(h, rel_bias, nsa_w_in[o], nsa_cmp_w1[o], nsa_cmp_b1[o], nsa_cmp_w2[o],
                            nsa_cmp_pe[o], nsa_w_out[o])
        x = x + gate1 * rmsnorm(out, norm_g[l, 1])
        h = rmsnorm(x, norm_g[l, 2]) * (1.0 + scale2) + shift2
        x = x + gate2 * rmsnorm(conv_ffn(h, ffn_w_up[l], ffn_conv_w[l], ffn_conv_b[l], ffn_w_down[l]), norm_g[l, 3])
    return x
```

```python
import functools
import math

import numpy as np
import jax
import jax.numpy as jnp
from jax import lax
from jax.experimental import pallas as pl
from jax.experimental.pallas import tpu as pltpu

BF16 = jnp.bfloat16
F32 = jnp.float32

SB_HEADS = 8
HEAD_DIM = 64
SB_WIDTH = SB_HEADS * HEAD_DIM
SSM_HEADS = 16
SSM_INNER = SSM_HEADS * HEAD_DIM
SSM_GROUPS = 4
SSM_STATE = 128
SSM_CONV = 4
SSM_CHUNK = 128
SSM_CONV_DIM = SSM_INNER + 2 * SSM_GROUPS * SSM_STATE
NSA_HEADS = 16
NSA_GROUPS = 4
NSA_HG = NSA_HEADS // NSA_GROUPS
NSA_Q = NSA_HEADS * HEAD_DIM
NSA_KV = NSA_GROUPS * HEAD_DIM
CMP_BLOCK = 32
CMP_STRIDE = 16
SEL_BLOCK = 64
N_SEL = 16
WINDOW = 512
N_BRANCH = 3
REL_BUCKETS = 32
REL_MAX_EXACT = 16
REL_MAX_DIST = 1024
FFN_CONV = 3
EPS = 1e-6
NEG = -1e30
FORCE = 1e4
SEL_MASK = 2.0 ** 100
M_INIT = -3.0e38

QT = 128
LANES = 128
VMEM_LIMIT = 48 * 1024 * 1024


def _cparams(*sem):
    return pltpu.CompilerParams(dimension_semantics=sem, vmem_limit_bytes=VMEM_LIMIT)


def _sigmoid(x):
    return 1.0 / (1.0 + jnp.exp(-x))


def _dot(a, b):
    return jnp.dot(a, b, preferred_element_type=F32)


def _dot_nt(a, b):
    return lax.dot_general(a, b, (((1,), (1,)), ((), ())), preferred_element_type=F32)


def _rms(y, g):
    ms = jnp.mean(y * y, axis=-1, keepdims=True)
    return y * lax.rsqrt(ms + EPS) * g


def _mod_kernel(c_ref, w_ref, b_ref, o_ref):
    c = c_ref[...]
    s = c * _sigmoid(c)
    o_ref[0] = _dot(s.astype(BF16), w_ref[0].astype(BF16)) + b_ref[0]


def _modulation(c, ada_w, ada_b):
    depth, d, n = ada_w.shape
    b = c.shape[0]
    bp = 8
    tn = 768
    cp = jnp.pad(c, ((0, bp - b), (0, 0)))
    out = pl.pallas_call(
        _mod_kernel,
        grid=(depth, n // tn),
        in_specs=[pl.BlockSpec((bp, d), lambda l, j: (0, 0)),
                  pl.BlockSpec((1, d, tn), lambda l, j: (l, 0, j)),
                  pl.BlockSpec((1, 1, tn), lambda l, j: (l, 0, j))],
        out_specs=pl.BlockSpec((1, bp, tn), lambda l, j: (l, 0, j)),
        out_shape=jax.ShapeDtypeStruct((depth, bp, n), F32),
        compiler_params=_cparams("parallel", "parallel"),
        name="adaln_mod",
    )(cp, ada_w, ada_b.reshape(depth, 1, n))
    return out[:, :b]


def _norm_proj_kernel(x_ref, g_ref, sc_ref, sh_ref, w_ref, o_ref, h_ref):
    @pl.when(pl.program_id(2) == 0)
    def _():
        y = _rms(x_ref[0], g_ref[...])
        h_ref[...] = (y * (1.0 + sc_ref[0]) + sh_ref[0]).astype(BF16)

    o_ref[0] = _dot(h_ref[...], w_ref[...]).astype(o_ref.dtype)


def _norm_proj(x, g, scale, shift, w, out_dtype, tn, name):
    b, s, d = x.shape
    n = w.shape[1]
    tm = min(512, s)
    return pl.pallas_call(
        _norm_proj_kernel,
        grid=(b, s // tm, n // tn),
        in_specs=[pl.BlockSpec((1, tm, d), lambda bi, i, j: (bi, i, 0)),
                  pl.BlockSpec((1, d), lambda bi, i, j: (0, 0)),
                  pl.BlockSpec((1, 1, d), lambda bi, i, j: (bi, 0, 0)),
                  pl.BlockSpec((1, 1, d), lambda bi, i, j: (bi, 0, 0)),
                  pl.BlockSpec((d, tn), lambda bi, i, j: (0, j))],
        out_specs=pl.BlockSpec((1, tm, tn), lambda bi, i, j: (bi, i, j)),
        out_shape=jax.ShapeDtypeStruct((b, s, n), out_dtype),
        scratch_shapes=[pltpu.VMEM((tm, d), BF16)],
        compiler_params=_cparams("parallel", "parallel", "arbitrary"),
        name=name,
    )(x, g.reshape(1, d), scale.reshape(b, 1, d), shift.reshape(b, 1, d), w)


def _out_proj_kernel(*refs, n_in):
    a_refs = refs[:n_in]
    w_refs = refs[n_in:2 * n_in]
    g_ref, gate_ref, x_ref, o_ref = refs[2 * n_in:]
    acc = _dot(a_refs[0][0], w_refs[0][...])
    for a_ref, w_ref in zip(a_refs[1:], w_refs[1:]):
        acc = acc + _dot(a_ref[0], w_ref[...])
    o_ref[0] = x_ref[0] + gate_ref[0] * _rms(acc, g_ref[...])


def _out_proj(acts, ws, g, gate, x, name):
    b, s, d = x.shape
    tm = min(512, s)
    n_in = len(acts)
    in_specs = [pl.BlockSpec((1, tm, a.shape[2]), lambda bi, i: (bi, i, 0)) for a in acts]
    in_specs += [pl.BlockSpec(w.shape, lambda bi, i: (0, 0)) for w in ws]
    in_specs += [pl.BlockSpec((1, d), lambda bi, i: (0, 0)),
                 pl.BlockSpec((1, 1, d), lambda bi, i: (bi, 0, 0)),
                 pl.BlockSpec((1, tm, d), lambda bi, i: (bi, i, 0))]
    return pl.pallas_call(
        functools.partial(_out_proj_kernel, n_in=n_in),
        grid=(b, s // tm),
        in_specs=in_specs,
        out_specs=pl.BlockSpec((1, tm, d), lambda bi, i: (bi, i, 0)),
        out_shape=jax.ShapeDtypeStruct((b, s, d), F32),
        compiler_params=_cparams("parallel", "parallel"),
        name=name,
    )(*acts, *ws, g.reshape(1, d), gate.reshape(b, 1, d), x)


def _shift_rows(cur, prev, k):
    sc = pltpu.roll(cur, k, 0)
    sp = pltpu.roll(prev, k, 0)
    row8 = lax.broadcasted_iota(jnp.int32, sp.shape, 0)
    head = jnp.where(row8 < k, sp, sc[:8])
    return jnp.concatenate([head, sc[8:]], axis=0)


def _gelu_tanh(x):
    return 0.5 * x * (1.0 + jnp.tanh(math.sqrt(2.0 / math.pi) * (x + 0.044715 * (x * x * x))))


def _ffn_down_kernel(u_ref, up_ref, cw_ref, cb_ref, w_ref, g_ref, gate_ref, x_ref, o_ref, a_ref, *, dff, ck):
    first = pl.program_id(1) == 0

    def conv(col0):
        cur = u_ref[0, :, col0:col0 + ck].astype(F32)
        prev = up_ref[0, :, col0:col0 + ck].astype(F32)
        prev = jnp.where(first, 0.0, prev)
        w = cw_ref[:, col0:col0 + ck]
        y = w[2:3] * cur + cb_ref[:, col0:col0 + ck]
        y = y + w[1:2] * _shift_rows(cur, prev, 1)
        return y + w[0:1] * _shift_rows(cur, prev, 2)

    for c in range(dff // ck):
        act = _gelu_tanh(conv(c * ck)) * conv(dff + c * ck)
        a_ref[:, c * ck:(c + 1) * ck] = act.astype(BF16)
    acc = _dot(a_ref[...], w_ref[...])
    o_ref[0] = x_ref[0] + gate_ref[0] * _rms(acc, g_ref[...])


def _ffn_down(u, conv_w, conv_b, w_down, g, gate, x):
    b, s, d = x.shape
    dff = w_down.shape[0]
    c2 = 2 * dff
    tm = min(256, s)
    cw = jnp.pad(conv_w, ((0, 8 - conv_w.shape[0]), (0, 0)))
    return pl.pallas_call(
        functools.partial(_ffn_down_kernel, dff=dff, ck=256),
        grid=(b, s // tm),
        in_specs=[pl.BlockSpec((1, tm, c2), lambda bi, i: (bi, i, 0)),
                  pl.BlockSpec((1, 8, c2), lambda bi, i: (bi, jnp.maximum(i * (tm // 8) - 1, 0), 0)),
                  pl.BlockSpec((8, c2), lambda bi, i: (0, 0)),
                  pl.BlockSpec((1, c2), lambda bi, i: (0, 0)),
                  pl.BlockSpec((dff, d), lambda bi, i: (0, 0)),
                  pl.BlockSpec((1, d), lambda bi, i: (0, 0)),
                  pl.BlockSpec((1, 1, d), lambda bi, i: (bi, 0, 0)),
                  pl.BlockSpec((1, tm, d), lambda bi, i: (bi, i, 0))],
        out_specs=pl.BlockSpec((1, tm, d), lambda bi, i: (bi, i, 0)),
        out_shape=jax.ShapeDtypeStruct((b, s, d), F32),
        scratch_shapes=[pltpu.VMEM((tm, dff), BF16)],
        compiler_params=_cparams("parallel", "arbitrary"),
        name="ffn_down",
    )(u, u, cw, conv_b.reshape(1, c2), w_down, g.reshape(1, d), gate.reshape(b, 1, d), x)


def _sb_kernel(q_ref, k_ref, v_ref, u_ref, o_ref):
    i = pl.program_id(2)
    q = q_ref[0, 0]
    row = lax.broadcasted_iota(jnp.int32, (QT, QT), 0)
    col = lax.broadcasted_iota(jnp.int32, (QT, QT), 1)
    below = col < row

    def tile(kidx, carry, diag):
        acc, csum = carry
        k0 = pl.multiple_of(kidx * QT, QT)
        k = k_ref[0, 0, pl.ds(k0, QT), :]
        v = v_ref[0, 0, pl.ds(k0, QT), :]
        z = _dot_nt(q, k)
        ls = jnp.minimum(z, 0.0) - jnp.log(1.0 + jnp.exp(-jnp.abs(z)))
        lr = ls - z
        if diag:
            lr = jnp.where(below, lr, 0.0)
        hi = lr.astype(BF16)
        lo = (lr - hi.astype(F32)).astype(BF16)
        cs = _dot(jnp.concatenate([hi, lo], axis=1), u_ref[...])
        w = jnp.exp(ls + cs[:, :QT] + csum)
        if diag:
            w = jnp.where(below, w, 0.0)
        acc = acc + _dot(w.astype(BF16), v)
        return acc, csum + cs[:, QT:]

    carry = (jnp.zeros((QT, HEAD_DIM), F32), jnp.zeros((QT, LANES), F32))
    carry = tile(i, carry, True)
    carry = lax.fori_loop(0, i, lambda it, c: tile(i - 1 - it, c, False), carry)
    o_ref[0, 0] = carry[0].astype(o_ref.dtype)


def _suffix_matrix():
    j = np.arange(2 * QT)[:, None] % QT
    s = np.arange(QT + LANES)[None, :]
    return jnp.asarray(np.where(s < QT, j > s, True).astype(np.float32), dtype=BF16)


def _sb_attention(q, k, v):
    b, h, s, d = q.shape
    return pl.pallas_call(
        _sb_kernel,
        grid=(b, h, s // QT),
        in_specs=[pl.BlockSpec((1, 1, QT, d), lambda bi, hi, i: (bi, hi, i, 0)),
                  pl.BlockSpec((1, 1, s, d), lambda bi, hi, i: (bi, hi, 0, 0)),
                  pl.BlockSpec((1, 1, s, d), lambda bi, hi, i: (bi, hi, 0, 0)),
                  pl.BlockSpec((2 * QT, QT + LANES), lambda bi, hi, i: (0, 0))],
        out_specs=pl.BlockSpec((1, 1, QT, d), lambda bi, hi, i: (bi, hi, i, 0)),
        out_shape=jax.ShapeDtypeStruct((b, h, s, d), BF16),
        compiler_params=_cparams("parallel", "parallel", "arbitrary"),
        name="sb_attention",
    )(q, k, v, _suffix_matrix())


def _pair(f, p, lane_lo):
    return jnp.where(lane_lo, f[:, 2 * p:2 * p + 1], f[:, 2 * p + 1:2 * p + 2])


def _ssd_kernel(xr_ref, xp_ref, dt_ref, z_ref, cw_ref, cb_ref, dtb_ref, alog_ref, dsk_ref, nrm_ref,
                o_ref, st_ref, xc_ref, y_ref):
    ln = SSM_CHUNK
    c = pl.program_id(1)

    @pl.when(c == 0)
    def _():
        st_ref[...] = jnp.zeros(st_ref.shape, F32)

    first = c == 0
    ck = 256
    for j in range(SSM_CONV_DIM // ck):
        cur = xr_ref[0, :, j * ck:(j + 1) * ck].astype(F32)
        prev = xp_ref[0, :, j * ck:(j + 1) * ck].astype(F32)
        prev = jnp.where(first, 0.0, prev)
        w = cw_ref[:, j * ck:(j + 1) * ck]
        y = w[3:4] * cur + cb_ref[:, j * ck:(j + 1) * ck]
        for kk in range(1, SSM_CONV):
            y = y + w[3 - kk:4 - kk] * _shift_rows(cur, prev, kk)
        xc_ref[:, j * ck:(j + 1) * ck] = y * _sigmoid(y)

    dtr = dt_ref[0] + dtb_ref[...]
    dt = jnp.maximum(dtr, 0.0) + jnp.log(1.0 + jnp.exp(-jnp.abs(dtr)))
    a = dt * (-jnp.exp(alog_ref[...]))
    row = lax.broadcasted_iota(jnp.int32, (ln, ln), 0)
    col = lax.broadcasted_iota(jnp.int32, (ln, ln), 1)
    causal = col <= row
    acs = jnp.dot(causal.astype(F32), a, precision=lax.Precision.HIGHEST, preferred_element_type=F32)
    acs_t = acs.T
    eacs = jnp.exp(acs)
    dout = jnp.exp(acs[ln - 1:ln, :] - acs)
    lane_lo = lax.broadcasted_iota(jnp.int32, (ln, LANES), 1) < HEAD_DIM

    hg = SSM_HEADS // SSM_GROUPS
    for g in range(SSM_GROUPS):
        bm = xc_ref[:, SSM_INNER + g * SSM_STATE:SSM_INNER + (g + 1) * SSM_STATE]
        cm = xc_ref[:, SSM_INNER + (SSM_GROUPS + g) * SSM_STATE:SSM_INNER + (SSM_GROUPS + g + 1) * SSM_STATE]
        cmb = cm.astype(BF16)
        cb = _dot_nt(cmb, bm.astype(BF16))
        prev = st_ref[g]
        yoff = _dot(cmb, prev.astype(BF16))
        dox_parts = []
        cdec_parts = []
        for pp in range(hg // 2):
            p = g * (hg // 2) + pp
            xs = xc_ref[:, p * LANES:(p + 1) * LANES]
            xdt = xs * _pair(dt, p, lane_lo)
            ms = []
            for hh in (2 * p, 2 * p + 1):
                seg = acs[:, hh:hh + 1] - acs_t[hh:hh + 1, :]
                decay = jnp.exp(jnp.where(causal, seg, NEG))
                ms.append((cb * decay).astype(BF16))
            rhs = jnp.concatenate([jnp.where(lane_lo, xdt, 0.0), jnp.where(lane_lo, 0.0, xdt)], axis=0)
            ydiag = _dot(jnp.concatenate(ms, axis=1), rhs.astype(BF16))
            yo = yoff[:, pp * LANES:(pp + 1) * LANES] * _pair(eacs, p, lane_lo)
            y_ref[:, p * LANES:(p + 1) * LANES] = ydiag + yo + xs * dsk_ref[:, p * LANES:(p + 1) * LANES]
            dox_parts.append((xdt * _pair(dout, p, lane_lo)).astype(BF16))
            cdec_parts.append(_pair(eacs[ln - 1:ln, :], p, lane_lo[0:1]))
        states = _dot(bm.T.astype(BF16), jnp.concatenate(dox_parts, axis=1))
        st_ref[g] = prev * jnp.concatenate(cdec_parts, axis=1) + states

    zf = z_ref[0].astype(F32)
    yz = y_ref[...] * (zf * _sigmoid(zf))
    o_ref[0] = _rms(yz, nrm_ref[...]).astype(o_ref.dtype)


def _ssd(proj, dt_raw, conv_w, conv_b, dt_bias, a_log, d_skip, ssm_norm):
    b, s, _ = proj.shape
    ln = SSM_CHUNK
    cw = jnp.pad(conv_w, ((0, 8 - SSM_CONV), (0, 0)))
    pad = LANES - SSM_HEADS
    return pl.pallas_call(
        _ssd_kernel,
        grid=(b, s // ln),
        in_specs=[pl.BlockSpec((1, ln, SSM_CONV_DIM), lambda bi, c: (bi, c, 0)),
                  pl.BlockSpec((1, 8, SSM_CONV_DIM), lambda bi, c: (bi, jnp.maximum(c * (ln // 8) - 1, 0), 0)),
                  pl.BlockSpec((1, ln, LANES), lambda bi, c: (bi, c, 0)),
                  pl.BlockSpec((1, ln, SSM_INNER), lambda bi, c: (bi, c, SSM_CONV_DIM // SSM_INNER)),
                  pl.BlockSpec((8, SSM_CONV_DIM), lambda bi, c: (0, 0)),
                  pl.BlockSpec((1, SSM_CONV_DIM), lambda bi, c: (0, 0)),
                  pl.BlockSpec((1, LANES), lambda bi, c: (0, 0)),
                  pl.BlockSpec((1, LANES), lambda bi, c: (0, 0)),
                  pl.BlockSpec((1, SSM_INNER), lambda bi, c: (0, 0)),
                  pl.BlockSpec((1, SSM_INNER), lambda bi, c: (0, 0))],
        out_specs=pl.BlockSpec((1, ln, SSM_INNER), lambda bi, c: (bi, c, 0)),
        out_shape=jax.ShapeDtypeStruct((b, s, SSM_INNER), BF16),
        scratch_shapes=[pltpu.VMEM((SSM_GROUPS, SSM_STATE, SSM_INNER // SSM_GROUPS), F32),
                        pltpu.VMEM((ln, SSM_CONV_DIM), F32),
                        pltpu.VMEM((ln, SSM_INNER), F32)],
        compiler_params=_cparams("parallel", "arbitrary"),
        name="ssd",
    )(proj, proj, dt_raw, proj, cw, conv_b.reshape(1, -1),
      jnp.pad(dt_bias, (0, pad)).reshape(1, LANES), jnp.pad(a_log, (0, pad)).reshape(1, LANES),
      jnp.repeat(d_skip, HEAD_DIM).reshape(1, SSM_INNER), ssm_norm.reshape(1, SSM_INNER))


def _rel_bucket(n):
    n = jnp.maximum(n, 0)
    nf = jnp.maximum(n, 1).astype(F32)
    large = REL_MAX_EXACT + (jnp.log(nf / REL_MAX_EXACT) / math.log(REL_MAX_DIST / REL_MAX_EXACT)
                             * (REL_BUCKETS - REL_MAX_EXACT)).astype(jnp.int32)
    large = jnp.minimum(large, REL_BUCKETS - 1)
    return jnp.where(n < REL_MAX_EXACT, n, large)


def _bias_lookup(rb_ref, head, bucket):
    out = jnp.full(bucket.shape, rb_ref[head, 0], F32)
    for kk in range(1, REL_BUCKETS):
        out = jnp.where(bucket >= kk, rb_ref[head, kk], out)
    return out


N_NEAR = 8


def _bias_tiles_kernel(rb_ref, o_ref):
    h = pl.program_id(0)
    row = lax.broadcasted_iota(jnp.int32, (QT, QT), 0)
    col = lax.broadcasted_iota(jnp.int32, (QT, QT), 1)
    for d in range(N_NEAR):
        bucket = _rel_bucket(d * QT + row - col)
        o_ref[0, d] = _bias_lookup(rb_ref, h, bucket) - rb_ref[h, REL_BUCKETS - 1]


def _bias_tiles(rel_bias):
    return pl.pallas_call(
        _bias_tiles_kernel,
        grid=(NSA_HEADS,),
        in_specs=[pl.BlockSpec(memory_space=pltpu.SMEM)],
        out_specs=pl.BlockSpec((1, N_NEAR, QT, QT), lambda h: (h, 0, 0, 0)),
        out_shape=jax.ShapeDtypeStruct((NSA_HEADS, N_NEAR, QT, QT), F32),
        compiler_params=_cparams("parallel"),
        name="nsa_bias_tiles",
    )(rel_bias)


def _compress_kernel(a_ref, w1_ref, b1_ref, w2_ref, pe_ref, o_ref):
    a = a_ref[0, 0, 0]
    ncp = a.shape[0]
    half = CMP_STRIDE * HEAD_DIM
    h1 = _dot(a, w1_ref[0, :half, :])
    h2 = _dot(a, w1_ref[0, half:, :])
    h2 = pltpu.roll(h2, ncp - 1, 0)
    rowi = lax.broadcasted_iota(jnp.int32, h2.shape, 0)
    h2 = jnp.where(rowi < ncp - 1, h2, 0.0)
    pev = _dot(pe_ref[0], w1_ref[0])[0:1]
    hid = h1 + h2 + pev + b1_ref[0]
    act = hid * _sigmoid(hid)
    o_ref[0, 0, 0] = _dot(act.astype(BF16), w2_ref[0]).astype(o_ref.dtype)


def _compress(a, w1, b1, w2, pe):
    _, b, g, ncp, kk = a.shape
    hid = w1.shape[2]
    pe8 = jnp.pad(pe.reshape(2, 1, CMP_BLOCK * HEAD_DIM), ((0, 0), (0, 7), (0, 0))).astype(BF16)
    return pl.pallas_call(
        _compress_kernel,
        grid=(2, b, g),
        in_specs=[pl.BlockSpec((1, 1, 1, ncp, kk), lambda t, bi, gi: (t, bi, gi, 0, 0)),
                  pl.BlockSpec((1, 2 * kk, hid), lambda t, bi, gi: (t, 0, 0)),
                  pl.BlockSpec((1, 1, hid), lambda t, bi, gi: (t, 0, 0)),
                  pl.BlockSpec((1, hid, HEAD_DIM), lambda t, bi, gi: (t, 0, 0)),
                  pl.BlockSpec((1, 8, 2 * kk), lambda t, bi, gi: (t, 0, 0))],
        out_specs=pl.BlockSpec((1, 1, 1, ncp, HEAD_DIM), lambda t, bi, gi: (t, bi, gi, 0, 0)),
        out_shape=jax.ShapeDtypeStruct((2, b, g, ncp, HEAD_DIM), BF16),
        compiler_params=_cparams("parallel", "parallel", "parallel"),
        name="nsa_compress",
    )(a, w1, b1.reshape(2, 1, hid), w2, pe8)


def _cattn_kernel(rb_ref, q_ref, k_ref, v_ref, ov_ref, oc_ref, sel_ref, bias_ref):
    g = pl.program_id(0)
    i = pl.program_id(1)
    ncp = k_ref.shape[2]
    q0 = i * QT
    trow = q0 + lax.broadcasted_iota(jnp.int32, (QT, ncp), 0)
    cend = lax.broadcasted_iota(jnp.int32, (QT, ncp), 1) * CMP_STRIDE + (CMP_BLOCK - 1)
    valid = cend <= trow

    @pl.when(pl.program_id(2) == 0)
    def _():
        bucket = _rel_bucket(trow - cend)
        for h in range(NSA_HG):
            bias_ref[h] = _bias_lookup(rb_ref, g * NSA_HG + h, bucket)

    q4 = q_ref[0, 0].reshape(NSA_HG * QT, HEAD_DIM)
    v = v_ref[0, 0]
    qk = _dot_nt(q4, k_ref[0, 0])
    pcsum = jnp.zeros((QT, ncp), F32)
    for h in range(NSA_HG):
        lg = jnp.where(valid, qk[h * QT:(h + 1) * QT] + bias_ref[h], NEG)
        m = jnp.max(lg, axis=1, keepdims=True)
        e = jnp.where(valid, jnp.exp(lg - m), 0.0)
        ssum = jnp.sum(e, axis=1, keepdims=True)
        p = e / jnp.where(ssum > 0.0, ssum, 1.0)
        pcsum = pcsum + p
        oc_ref[0, 0, h] = _dot(p.astype(BF16), v).astype(oc_ref.dtype)

    hi = pcsum.astype(BF16)
    lo = (pcsum - hi.astype(F32)).astype(BF16)
    imp = _dot_nt(ov_ref[...], hi) + _dot_nt(ov_ref[...], lo)
    jrow = lax.broadcasted_iota(jnp.int32, (LANES, QT), 0)
    tq = q0 + lax.broadcasted_iota(jnp.int32, (LANES, QT), 1)
    cur = jnp.right_shift(tq, SEL_BLOCK.bit_length() - 1)
    forced = (jrow == 0) | (jrow == cur) | (jrow == cur - 1)
    val = jnp.where(forced, FORCE, jnp.where(jrow * SEL_BLOCK <= tq, imp, -1.0))
    jf = jrow.astype(F32)
    sel = jnp.zeros((LANES, QT), F32)
    for _ in range(N_SEL):
        m = jnp.max(val, axis=0, keepdims=True)
        jm = jnp.min(jnp.where(val == m, jf, 1e9), axis=0, keepdims=True)
        hit = jf == jm
        sel = jnp.where(hit, 1.0, sel)
        val = jnp.where(hit, M_INIT, val)
    sel_ref[0, 0] = ((sel.T - 1.0) * SEL_MASK).astype(sel_ref.dtype)


def _overlap_t(ncp):
    ci = (np.arange(ncp) * CMP_STRIDE)[None, :]
    sj = (np.arange(LANES) * SEL_BLOCK)[:, None]
    return jnp.asarray(((ci < sj + SEL_BLOCK) & (ci + CMP_BLOCK > sj)).astype(np.float32), dtype=BF16)


def _cattn(rel_bias, q, kcmp, vcmp):
    b, g, hg, s, d = q.shape
    ncp = kcmp.shape[2]
    nq = s // QT
    return pl.pallas_call(
        _cattn_kernel,
        grid=(g, nq, b),
        in_specs=[pl.BlockSpec(memory_space=pltpu.SMEM),
                  pl.BlockSpec((1, 1, hg, QT, d), lambda gi, i, bi: (bi, gi, 0, i, 0)),
                  pl.BlockSpec((1, 1, ncp, d), lambda gi, i, bi: (bi, gi, 0, 0)),
                  pl.BlockSpec((1, 1, ncp, d), lambda gi, i, bi: (bi, gi, 0, 0)),
                  pl.BlockSpec((LANES, ncp), lambda gi, i, bi: (0, 0))],
        out_specs=[pl.BlockSpec((1, 1, hg, QT, d), lambda gi, i, bi: (bi, gi, 0, i, 0)),
                   pl.BlockSpec((1, 1, QT, LANES), lambda gi, i, bi: (bi, gi, i, 0))],
        out_shape=[jax.ShapeDtypeStruct((b, g, hg, s, d), BF16),
                   jax.ShapeDtypeStruct((b, g, s, LANES), BF16)],
        scratch_shapes=[pltpu.VMEM((hg, QT, ncp), F32)],
        compiler_params=_cparams("parallel", "parallel", "arbitrary"),
        name="nsa_cmp_attn",
    )(rel_bias, q, kcmp, vcmp, _overlap_t(ncp))


def _sattn_kernel(q_ref, sel_ref, ks_ref, vs_ref, kw_ref, vw_ref, tt_ref, oc_ref, gt_ref, o_ref,
                  m_ref, l_ref, acc_ref):
    i = pl.program_id(2)
    rows = NSA_HG * QT
    q4 = q_ref[0, 0].reshape(rows, HEAD_DIM)
    selb = sel_ref[0, 0]
    qa = jnp.concatenate([q4, jnp.concatenate([selb] * NSA_HG, axis=0)], axis=1)
    row = lax.broadcasted_iota(jnp.int32, (rows, QT), 0) % QT
    col = lax.broadcasted_iota(jnp.int32, (rows, QT), 1)

    def reset():
        m_ref[...] = jnp.full(m_ref.shape, M_INIT, F32)
        l_ref[...] = jnp.zeros(l_ref.shape, F32)
        acc_ref[...] = jnp.zeros(acc_ref.shape, F32)

    def update(s, v):
        m_prev = m_ref[...]
        m_new = jnp.maximum(m_prev, jnp.max(s, axis=1, keepdims=True))
        alpha = jnp.exp(m_prev - m_new)
        p = jnp.exp(s - m_new)
        l_ref[...] = alpha * l_ref[...] + jnp.sum(p, axis=1, keepdims=True)
        acc_ref[...] = alpha[:, :HEAD_DIM] * acc_ref[...] + _dot(p.astype(BF16), v)
        m_ref[...] = m_new

    def near_bias(d):
        return jnp.concatenate([tt_ref[h, d] for h in range(NSA_HG)], axis=0)

    def finish():
        return acc_ref[...] / l_ref[:, :HEAD_DIM]

    def sel_tile(kt, d):
        k0 = pl.multiple_of(kt * QT, QT)
        s = _dot_nt(qa, ks_ref[0, 0, pl.ds(k0, QT), :])
        if d is not None:
            s = s + near_bias(d)
        return s, vs_ref[0, 0, pl.ds(k0, QT), :]

    reset()
    n_far = jnp.maximum(i - (N_NEAR - 1), 0)

    def far_body(kt, carry):
        update(*sel_tile(kt, None))
        return carry

    def near_body(kt, carry):
        update(*sel_tile(kt, i - kt))
        return carry

    lax.fori_loop(0, n_far, far_body, 0)
    lax.fori_loop(n_far, i, near_body, 0)
    s, v = sel_tile(i, 0)
    update(jnp.where(col <= row, s, NEG), v)
    osl = finish()

    reset()
    nwin = WINDOW // QT
    for d in range(nwin, -1, -1):
        @pl.when(i - d >= 0)
        def _(d=d):
            k0 = pl.multiple_of((i - d) * QT, QT)
            s = _dot_nt(q4, kw_ref[0, 0, pl.ds(k0, QT), :]) + near_bias(d)
            if d == nwin:
                s = jnp.where(col > row, s, NEG)
            if d == 0:
                s = jnp.where(col <= row, s, NEG)
            update(s, vw_ref[0, 0, pl.ds(k0, QT), :])
    ow = finish()

    gt = _sigmoid(gt_ref[0, 0])
    for h in range(NSA_HG):
        sl = slice(h * QT, (h + 1) * QT)
        o = (gt[:, 3 * h:3 * h + 1] * oc_ref[0, 0, h].astype(F32)
             + gt[:, 3 * h + 1:3 * h + 2] * osl[sl] + gt[:, 3 * h + 2:3 * h + 3] * ow[sl])
        o_ref[0, 0, h] = o.astype(o_ref.dtype)


def _sattn(q, selb, ksa, vs, kw, vw, tt, oc, gates):
    b, g, hg, s, d = q.shape
    nq = s // QT
    ka = ksa.shape[3]
    kv_spec = lambda width: pl.BlockSpec((1, 1, s, width), lambda bi, gi, i: (bi, gi, 0, 0))
    q_spec = pl.BlockSpec((1, 1, hg, QT, d), lambda bi, gi, i: (bi, gi, 0, i, 0))
    return pl.pallas_call(
        _sattn_kernel,
        grid=(b, g, nq),
        in_specs=[q_spec,
                  pl.BlockSpec((1, 1, QT, LANES), lambda bi, gi, i: (bi, gi, i, 0)),
                  kv_spec(ka), kv_spec(d), kv_spec(d), kv_spec(d),
                  pl.BlockSpec((hg, N_NEAR, QT, QT), lambda bi, gi, i: (gi, 0, 0, 0)),
                  q_spec,
                  pl.BlockSpec((1, 1, QT, hg * N_BRANCH), lambda bi, gi, i: (bi, gi, i, 0))],
        out_specs=q_spec,
        out_shape=jax.ShapeDtypeStruct((b, g, hg, s, d), BF16),
        scratch_shapes=[pltpu.VMEM((hg * QT, LANES), F32),
                        pltpu.VMEM((hg * QT, LANES), F32),
                        pltpu.VMEM((hg * QT, d), F32)],
        compiler_params=_cparams("parallel", "parallel", "arbitrary"),
        name="nsa_sel_win_attn",
    )(q, selb, ksa, vs, kw, vw, tt, oc, gates)


def _heads(t, h):
    b, s, _ = t.shape
    return t.reshape(b, s, h, HEAD_DIM).transpose(0, 2, 1, 3)


def _mixer_ab(x, g, scale, shift, w_in, conv_w, conv_b, dt_bias, a_log, d_skip, ssm_norm):
    b, s, d = x.shape
    o3 = 3 * SB_WIDTH
    qs = HEAD_DIM ** -0.5
    w_q, w_k, w_v = w_in[:, :SB_WIDTH] * qs, w_in[:, SB_WIDTH:2 * SB_WIDTH], w_in[:, 2 * SB_WIDTH:o3]
    w_z = w_in[:, o3:o3 + SSM_INNER]
    w_xbc = w_in[:, o3 + SSM_INNER:o3 + SSM_INNER + SSM_CONV_DIM]
    w_dt = w_in[:, o3 + SSM_INNER + SSM_CONV_DIM:]
    w_main = jnp.concatenate([w_xbc, w_z, w_q, w_k, w_v], axis=1).astype(BF16)
    w_dt = jnp.pad(w_dt, ((0, 0), (0, LANES - SSM_HEADS))).astype(BF16)
    proj = _norm_proj(x, g, scale, shift, w_main, BF16, 512, "ab_in_proj")
    dt_raw = _norm_proj(x, g, scale, shift, w_dt, F32, LANES, "ab_dt_proj")
    c0 = SSM_CONV_DIM + SSM_INNER
    q = _heads(proj[..., c0:c0 + SB_WIDTH], SB_HEADS)
    k = _heads(proj[..., c0 + SB_WIDTH:c0 + 2 * SB_WIDTH], SB_HEADS)
    v = _heads(proj[..., c0 + 2 * SB_WIDTH:c0 + 3 * SB_WIDTH], SB_HEADS)
    o_sb = _sb_attention(q, k, v).transpose(0, 2, 1, 3).reshape(b, s, SB_WIDTH)
    y = _ssd(proj, dt_raw, conv_w, conv_b, dt_bias, a_log, d_skip, ssm_norm)
    return o_sb, y


def _mixer_nsa(x, g, scale, shift, tt, rel_bias, w_in, cmp_w1, cmp_b1, cmp_w2, cmp_pe):
    b, s, d = x.shape
    gq, hg = NSA_GROUPS, NSA_HG
    qs = HEAD_DIM ** -0.5
    n_main = NSA_Q + 6 * NSA_KV
    w_main = jnp.concatenate([w_in[:, :NSA_Q] * qs, w_in[:, NSA_Q:n_main]], axis=1).astype(BF16)
    n_gate = NSA_HEADS * N_BRANCH
    w_gate = jnp.pad(w_in[:, n_main:], ((0, 0), (0, LANES - n_gate))).astype(BF16)
    proj = _norm_proj(x, g, scale, shift, w_main, BF16, 512, "nsa_in_proj")
    gates = _norm_proj(x, g, scale, shift, w_gate, F32, LANES, "nsa_gate_proj")[..., :n_gate]
    gates = gates.reshape(b, s, gq, hg * N_BRANCH).transpose(0, 2, 1, 3)
    q = proj[..., :NSA_Q].reshape(b, s, gq, hg, HEAD_DIM).transpose(0, 2, 3, 1, 4)
    part = lambda j: proj[..., NSA_Q + j * NSA_KV:NSA_Q + (j + 1) * NSA_KV]
    nc = s // CMP_STRIDE
    kvc = jnp.stack([part(0), part(1)])
    a = kvc.reshape(2, b, nc, CMP_STRIDE, gq, HEAD_DIM).transpose(0, 1, 4, 2, 3, 5)
    a = a.reshape(2, b, gq, nc, CMP_STRIDE * HEAD_DIM)
    cmp = _compress(a, cmp_w1.astype(BF16), cmp_b1, cmp_w2.astype(BF16), cmp_pe)
    oc, selb = _cattn(rel_bias, q, cmp[0], cmp[1])
    grp = lambda t: t.reshape(b, s, gq, HEAD_DIM).transpose(0, 2, 1, 3)
    onehot = (jnp.arange(s)[:, None] // SEL_BLOCK == jnp.arange(LANES)[None, :]).astype(BF16)
    ksa = jnp.concatenate([grp(part(2)), jnp.broadcast_to(onehot, (b, gq, s, LANES))], axis=-1)
    o = _sattn(q, selb, ksa, grp(part(3)), grp(part(4)), grp(part(5)),
               tt, oc, gates)
    return o.transpose(0, 3, 1, 2, 4).reshape(b, s, NSA_Q)


def kernel(x, c, rel_bias, ada_w, ada_b, norm_g, ab_w_in, ab_conv_w, ab_conv_b, ab_dt_bias, ab_a_log,
           ab_d_skip, ab_ssm_norm, ab_w_out, nsa_w_in, nsa_cmp_w1, nsa_cmp_b1, nsa_cmp_w2, nsa_cmp_pe,
           nsa_w_out, ffn_w_up, ffn_conv_w, ffn_conv_b, ffn_w_down):
    depth = ada_w.shape[0]
    d = x.shape[-1]
    assert x.shape[1] % 1024 == 0 and x.shape[1] // SEL_BLOCK <= LANES
    mod = _modulation(c, ada_w, ada_b)
    tt = _bias_tiles(rel_bias) if depth > 1 else None
    for l in range(depth):
        shift1, scale1, gate1, shift2, scale2, gate2 = [mod[l, :, j * d:(j + 1) * d] for j in range(6)]
        if l % 2 == 0:
            e = l // 2
            o_sb, y = _mixer_ab(x, norm_g[l, 0], scale1, shift1, ab_w_in[e], ab_conv_w[e], ab_conv_b[e],
                                ab_dt_bias[e], ab_a_log[e], ab_d_skip[e], ab_ssm_norm[e])
            w_out = ab_w_out[e].astype(BF16)
            x = _out_proj([o_sb, y], [w_out[:SB_WIDTH], w_out[SB_WIDTH:]], norm_g[l, 1], gate1, x, "ab_out_proj")
        else:
            o = l // 2
            a = _mixer_nsa(x, norm_g[l, 0], scale1, shift1, tt, rel_bias, nsa_w_in[o], nsa_cmp_w1[o],
                           nsa_cmp_b1[o], nsa_cmp_w2[o], nsa_cmp_pe[o])
            x = _out_proj([a], [nsa_w_out[o].astype(BF16)], norm_g[l, 1], gate1, x, "nsa_out_proj")
        u = _norm_proj(x, norm_g[l, 2], scale2, shift2, ffn_w_up[l].astype(BF16), BF16, 512, "ffn_up_proj")
        x = _ffn_down(u, ffn_conv_w[l], ffn_conv_b[l], ffn_w_down[l].astype(BF16), norm_g[l, 3], gate2, x)
    return x
```

```python
import functools
import math

import numpy as np
import jax
import jax.numpy as jnp
from jax import lax
from jax.experimental import pallas as pl
from jax.experimental.pallas import tpu as pltpu

BF16 = jnp.bfloat16
F32 = jnp.float32

SB_HEADS = 8
HEAD_DIM = 64
SB_WIDTH = SB_HEADS * HEAD_DIM
SSM_HEADS = 16
SSM_INNER = SSM_HEADS * HEAD_DIM
SSM_GROUPS = 4
SSM_STATE = 128
SSM_CONV = 4
SSM_CHUNK = 128
SSM_CONV_DIM = SSM_INNER + 2 * SSM_GROUPS * SSM_STATE
NSA_HEADS = 16
NSA_GROUPS = 4
NSA_HG = NSA_HEADS // NSA_GROUPS
NSA_Q = NSA_HEADS * HEAD_DIM
NSA_KV = NSA_GROUPS * HEAD_DIM
CMP_BLOCK = 32
CMP_STRIDE = 16
SEL_BLOCK = 64
N_SEL = 16
WINDOW = 512
N_BRANCH = 3
REL_BUCKETS = 32
REL_MAX_EXACT = 16
REL_MAX_DIST = 1024
FFN_CONV = 3
EPS = 1e-6
NEG = -1e30
FORCE = 1e4
SEL_MASK = 2.0 ** 100
M_INIT = -3.0e38

QT = 128
LANES = 128
VMEM_LIMIT = 48 * 1024 * 1024


def _cparams(*sem):
    return pltpu.CompilerParams(dimension_semantics=sem, vmem_limit_bytes=VMEM_LIMIT)


def _sigmoid(x):
    return 1.0 / (1.0 + jnp.exp(-x))


def _dot(a, b):
    return jnp.dot(a, b, preferred_element_type=F32)


def _dot_nt(a, b):
    return lax.dot_general(a, b, (((1,), (1,)), ((), ())), preferred_element_type=F32)


def _rms(y, g):
    ms = jnp.mean(y * y, axis=-1, keepdims=True)
    return y * lax.rsqrt(ms + EPS) * g


def _mod_kernel(c_ref, w_ref, b_ref, o_ref):
    c = c_ref[...]
    s = c * _sigmoid(c)
    o_ref[0] = _dot(s.astype(BF16), w_ref[0].astype(BF16)) + b_ref[0]


def _modulation(c, ada_w, ada_b):
    depth, d, n = ada_w.shape
    b = c.shape[0]
    bp = 8
    tn = 768
    cp = jnp.pad(c, ((0, bp - b), (0, 0)))
    out = pl.pallas_call(
        _mod_kernel,
        grid=(depth, n // tn),
        in_specs=[pl.BlockSpec((bp, d), lambda l, j: (0, 0)),
                  pl.BlockSpec((1, d, tn), lambda l, j: (l, 0, j)),
                  pl.BlockSpec((1, 1, tn), lambda l, j: (l, 0, j))],
        out_specs=pl.BlockSpec((1, bp, tn), lambda l, j: (l, 0, j)),
        out_shape=jax.ShapeDtypeStruct((depth, bp, n), F32),
        compiler_params=_cparams("parallel", "parallel"),
        name="adaln_mod",
    )(cp, ada_w, ada_b.reshape(depth, 1, n))
    return out[:, :b]


def _norm_proj_kernel(x_ref, g_ref, sc_ref, sh_ref, w_ref, o_ref, h_ref):
    @pl.when(pl.program_id(2) == 0)
    def _():
        y = _rms(x_ref[0], g_ref[...])
        h_ref[...] = (y * (1.0 + sc_ref[0]) + sh_ref[0]).astype(BF16)

    o_ref[0] = _dot(h_ref[...], w_ref[...]).astype(o_ref.dtype)


def _norm_proj(x, g, scale, shift, w, out_dtype, tn, name):
    b, s, d = x.shape
    n = w.shape[1]
    tm = min(512, s)
    return pl.pallas_call(
        _norm_proj_kernel,
        grid=(b, s // tm, n // tn),
        in_specs=[pl.BlockSpec((1, tm, d), lambda bi, i, j: (bi, i, 0)),
                  pl.BlockSpec((1, d), lambda bi, i, j: (0, 0)),
                  pl.BlockSpec((1, 1, d), lambda bi, i, j: (bi, 0, 0)),
                  pl.BlockSpec((1, 1, d), lambda bi, i, j: (bi, 0, 0)),
                  pl.BlockSpec((d, tn), lambda bi, i, j: (0, j))],
        out_specs=pl.BlockSpec((1, tm, tn), lambda bi, i, j: (bi, i, j)),
        out_shape=jax.ShapeDtypeStruct((b, s, n), out_dtype),
        scratch_shapes=[pltpu.VMEM((tm, d), BF16)],
        compiler_params=_cparams("parallel", "parallel", "arbitrary"),
        name=name,
    )(x, g.reshape(1, d), scale.reshape(b, 1, d), shift.reshape(b, 1, d), w)


def _out_proj_kernel(*refs, n_in):
    a_refs = refs[:n_in]
    w_refs = refs[n_in:2 * n_in]
    g_ref, gate_ref, x_ref, o_ref = refs[2 * n_in:]
    acc = _dot(a_refs[0][0], w_refs[0][...])
    for a_ref, w_ref in zip(a_refs[1:], w_refs[1:]):
        acc = acc + _dot(a_ref[0], w_ref[...])
    o_ref[0] = x_ref[0] + gate_ref[0] * _rms(acc, g_ref[...])


def _out_proj(acts, ws, g, gate, x, name):
    b, s, d = x.shape
    tm = min(512, s)
    n_in = len(acts)
    in_specs = [pl.BlockSpec((1, tm, a.shape[2]), lambda bi, i: (bi, i, 0)) for a in acts]
    in_specs += [pl.BlockSpec(w.shape, lambda bi, i: (0, 0)) for w in ws]
    in_specs += [pl.BlockSpec((1, d), lambda bi, i: (0, 0)),
                 pl.BlockSpec((1, 1, d), lambda bi, i: (bi, 0, 0)),
                 pl.BlockSpec((1, tm, d), lambda bi, i: (bi, i, 0))]
    return pl.pallas_call(
        functools.partial(_out_proj_kernel, n_in=n_in),
        grid=(b, s // tm),
        in_specs=in_specs,
        out_specs=pl.BlockSpec((1, tm, d), lambda bi, i: (bi, i, 0)),
        out_shape=jax.ShapeDtypeStruct((b, s, d), F32),
        compiler_params=_cparams("parallel", "parallel"),
        name=name,
    )(*acts, *ws, g.reshape(1, d), gate.reshape(b, 1, d), x)


def _shift_rows(cur, prev, k):
    sc = pltpu.roll(cur, k, 0)
    sp = pltpu.roll(prev, k, 0)
    row8 = lax.broadcasted_iota(jnp.int32, sp.shape, 0)
    head = jnp.where(row8 < k, sp, sc[:8])
    return jnp.concatenate([head, sc[8:]], axis=0)


def _gelu_tanh(x):
    return 0.5 * x * (1.0 + jnp.tanh(math.sqrt(2.0 / math.pi) * (x + 0.044715 * (x * x * x))))


def _ffn_down_kernel(u_ref, up_ref, cw_ref, cb_ref, w_ref, g_ref, gate_ref, x_ref, o_ref, a_ref, *, dff, ck):
    first = pl.program_id(1) == 0

    def conv(col0):
        cur = u_ref[0, :, col0:col0 + ck].astype(F32)
        prev = up_ref[0, :, col0:col0 + ck].astype(F32)
        prev = jnp.where(first, 0.0, prev)
        w = cw_ref[:, col0:col0 + ck]
        y = w[2:3] * cur + cb_ref[:, col0:col0 + ck]
        y = y + w[1:2] * _shift_rows(cur, prev, 1)
        return y + w[0:1] * _shift_rows(cur, prev, 2)

    for c in range(dff // ck):
        act = _gelu_tanh(conv(c * ck)) * conv(dff + c * ck)
        a_ref[:, c * ck:(c + 1) * ck] = act.astype(BF16)
    acc = _dot(a_ref[...], w_ref[...])
    o_ref[0] = x_ref[0] + gate_ref[0] * _rms(acc, g_ref[...])


def _ffn_down(u, conv_w, conv_b, w_down, g, gate, x):
    b, s, d = x.shape
    dff = w_down.shape[0]
    c2 = 2 * dff
    tm = min(256, s)
    cw = jnp.pad(conv_w, ((0, 8 - conv_w.shape[0]), (0, 0)))
    return pl.pallas_call(
        functools.partial(_ffn_down_kernel, dff=dff, ck=256),
        grid=(b, s // tm),
        in_specs=[pl.BlockSpec((1, tm, c2), lambda bi, i: (bi, i, 0)),
                  pl.BlockSpec((1, 8, c2), lambda bi, i: (bi, jnp.maximum(i * (tm // 8) - 1, 0), 0)),
                  pl.BlockSpec((8, c2), lambda bi, i: (0, 0)),
                  pl.BlockSpec((1, c2), lambda bi, i: (0, 0)),
                  pl.BlockSpec((dff, d), lambda bi, i: (0, 0)),
                  pl.BlockSpec((1, d), lambda bi, i: (0, 0)),
                  pl.BlockSpec((1, 1, d), lambda bi, i: (bi, 0, 0)),
                  pl.BlockSpec((1, tm, d), lambda bi, i: (bi, i, 0))],
        out_specs=pl.BlockSpec((1, tm, d), lambda bi, i: (bi, i, 0)),
        out_shape=jax.ShapeDtypeStruct((b, s, d), F32),
        scratch_shapes=[pltpu.VMEM((tm, dff), BF16)],
        compiler_params=_cparams("parallel", "arbitrary"),
        name="ffn_down",
    )(u, u, cw, conv_b.reshape(1, c2), w_down, g.reshape(1, d), gate.reshape(b, 1, d), x)


SBQ = 512


def _sb_kernel(q_ref, k_ref, v_ref, u_ref, o_ref, acc_ref, cs_ref):
    i = pl.program_id(2)
    q = q_ref[0, 0]
    nsub = SBQ // QT
    diff = (lax.broadcasted_iota(jnp.int32, (SBQ, QT), 1) - lax.broadcasted_iota(jnp.int32, (SBQ, QT), 0))
    acc_ref[...] = jnp.zeros(acc_ref.shape, F32)
    cs_ref[...] = jnp.zeros(cs_ref.shape, F32)

    def tiles(kidxs, diag):
        offs = [pl.multiple_of(kidx * QT, QT) for kidx in kidxs]
        zs = [_dot_nt(q, k_ref[0, 0, pl.ds(k0, QT), :]) for k0 in offs]
        lss, css, belows = [], [], []
        for kidx, z in zip(kidxs, zs):
            ls = jnp.minimum(z, 0.0) - jnp.log(1.0 + jnp.exp(-jnp.abs(z)))
            lr = ls - z
            below = None
            if diag:
                below = diff < i * SBQ - kidx * QT
                lr = jnp.where(below, lr, 0.0)
            hi = lr.astype(BF16)
            lo = (lr - hi.astype(F32)).astype(BF16)
            css.append(_dot(jnp.concatenate([hi, lo], axis=1), u_ref[...]))
            lss.append(ls)
            belows.append(below)
        csum = cs_ref[...]
        acc = acc_ref[...]
        for k0, ls, cs, below in zip(offs, lss, css, belows):
            w = jnp.exp(ls + cs[:, :QT] + csum)
            if diag:
                w = jnp.where(below, w, 0.0)
            acc = acc + _dot(w.astype(BF16), v_ref[0, 0, pl.ds(k0, QT), :])
            csum = csum + cs[:, QT:]
        acc_ref[...] = acc
        cs_ref[...] = csum

    tiles([i * nsub + r for r in range(nsub - 1, -1, -1)], True)

    def body(it, carry):
        tiles([(i - it) * nsub - 1 - r for r in range(nsub)], False)
        return carry

    lax.fori_loop(0, i, body, 0)
    o_ref[0, 0] = acc_ref[...].astype(o_ref.dtype)


def _suffix_matrix():
    j = np.arange(2 * QT)[:, None] % QT
    s = np.arange(QT + LANES)[None, :]
    return jnp.asarray(np.where(s < QT, j > s, True).astype(np.float32), dtype=BF16)


def _sb_attention(q, k, v):
    b, h, s, d = q.shape
    return pl.pallas_call(
        _sb_kernel,
        grid=(b, h, s // SBQ),
        in_specs=[pl.BlockSpec((1, 1, SBQ, d), lambda bi, hi, i: (bi, hi, i, 0)),
                  pl.BlockSpec((1, 1, s, d), lambda bi, hi, i: (bi, hi, 0, 0)),
                  pl.BlockSpec((1, 1, s, d), lambda bi, hi, i: (bi, hi, 0, 0)),
                  pl.BlockSpec((2 * QT, QT + LANES), lambda bi, hi, i: (0, 0))],
        out_specs=pl.BlockSpec((1, 1, SBQ, d), lambda bi, hi, i: (bi, hi, i, 0)),
        out_shape=jax.ShapeDtypeStruct((b, h, s, d), BF16),
        scratch_shapes=[pltpu.VMEM((SBQ, d), F32), pltpu.VMEM((SBQ, LANES), F32)],
        compiler_params=_cparams("parallel", "parallel", "arbitrary"),
        name="sb_attention",
    )(q, k, v, _suffix_matrix())


def _pair(f, p, lane_lo):
    return jnp.where(lane_lo, f[:, 2 * p:2 * p + 1], f[:, 2 * p + 1:2 * p + 2])


def _ssd_kernel(xr_ref, xp_ref, dt_ref, z_ref, cw_ref, cb_ref, dtb_ref, alog_ref, dsk_ref, nrm_ref,
                o_ref, st_ref, xc_ref, y_ref):
    ln = SSM_CHUNK
    c = pl.program_id(1)

    @pl.when(c == 0)
    def _():
        st_ref[...] = jnp.zeros(st_ref.shape, F32)

    first = c == 0
    ck = 256
    for j in range(SSM_CONV_DIM // ck):
        cur = xr_ref[0, :, j * ck:(j + 1) * ck].astype(F32)
        prev = xp_ref[0, :, j * ck:(j + 1) * ck].astype(F32)
        prev = jnp.where(first, 0.0, prev)
        w = cw_ref[:, j * ck:(j + 1) * ck]
        y = w[3:4] * cur + cb_ref[:, j * ck:(j + 1) * ck]
        for kk in range(1, SSM_CONV):
            y = y + w[3 - kk:4 - kk] * _shift_rows(cur, prev, kk)
        xc_ref[:, j * ck:(j + 1) * ck] = y * _sigmoid(y)

    dtr = dt_ref[0] + dtb_ref[...]
    dt = jnp.maximum(dtr, 0.0) + jnp.log(1.0 + jnp.exp(-jnp.abs(dtr)))
    a = dt * (-jnp.exp(alog_ref[...]))
    row = lax.broadcasted_iota(jnp.int32, (ln, ln), 0)
    col = lax.broadcasted_iota(jnp.int32, (ln, ln), 1)
    causal = col <= row
    acs = jnp.dot(causal.astype(F32), a, precision=lax.Precision.HIGHEST, preferred_element_type=F32)
    acs_t = acs.T
    eacs = jnp.exp(acs)
    dout = jnp.exp(acs[ln - 1:ln, :] - acs)
    lane_lo = lax.broadcasted_iota(jnp.int32, (ln, LANES), 1) < HEAD_DIM

    hg = SSM_HEADS // SSM_GROUPS
    for g in range(SSM_GROUPS):
        bm = xc_ref[:, SSM_INNER + g * SSM_STATE:SSM_INNER + (g + 1) * SSM_STATE]
        cm = xc_ref[:, SSM_INNER + (SSM_GROUPS + g) * SSM_STATE:SSM_INNER + (SSM_GROUPS + g + 1) * SSM_STATE]
        cmb = cm.astype(BF16)
        cb = _dot_nt(cmb, bm.astype(BF16))
        prev = st_ref[g]
        yoff = _dot(cmb, prev.astype(BF16))
        dox_parts = []
        cdec_parts = []
        for pp in range(hg // 2):
            p = g * (hg // 2) + pp
            xs = xc_ref[:, p * LANES:(p + 1) * LANES]
            xdt = xs * _pair(dt, p, lane_lo)
            ms = []
            for hh in (2 * p, 2 * p + 1):
                seg = acs[:, hh:hh + 1] - acs_t[hh:hh + 1, :]
                decay = jnp.exp(jnp.where(causal, seg, NEG))
                ms.append((cb * decay).astype(BF16))
            rhs = jnp.concatenate([jnp.where(lane_lo, xdt, 0.0), jnp.where(lane_lo, 0.0, xdt)], axis=0)
            ydiag = _dot(jnp.concatenate(ms, axis=1), rhs.astype(BF16))
            yo = yoff[:, pp * LANES:(pp + 1) * LANES] * _pair(eacs, p, lane_lo)
            y_ref[:, p * LANES:(p + 1) * LANES] = ydiag + yo + xs * dsk_ref[:, p * LANES:(p + 1) * LANES]
            dox_parts.append((xdt * _pair(dout, p, lane_lo)).astype(BF16))
            cdec_parts.append(_pair(eacs[ln - 1:ln, :], p, lane_lo[0:1]))
        states = _dot(bm.T.astype(BF16), jnp.concatenate(dox_parts, axis=1))
        st_ref[g] = prev * jnp.concatenate(cdec_parts, axis=1) + states

    zf = z_ref[0].astype(F32)
    yz = y_ref[...] * (zf * _sigmoid(zf))
    o_ref[0] = _rms(yz, nrm_ref[...]).astype(o_ref.dtype)


def _ssd(proj, dt_raw, conv_w, conv_b, dt_bias, a_log, d_skip, ssm_norm):
    b, s, _ = proj.shape
    ln = SSM_CHUNK
    cw = jnp.pad(conv_w, ((0, 8 - SSM_CONV), (0, 0)))
    pad = LANES - SSM_HEADS
    return pl.pallas_call(
        _ssd_kernel,
        grid=(b, s // ln),
        in_specs=[pl.BlockSpec((1, ln, SSM_CONV_DIM), lambda bi, c: (bi, c, 0)),
                  pl.BlockSpec((1, 8, SSM_CONV_DIM), lambda bi, c: (bi, jnp.maximum(c * (ln // 8) - 1, 0), 0)),
                  pl.BlockSpec((1, ln, LANES), lambda bi, c: (bi, c, 0)),
                  pl.BlockSpec((1, ln, SSM_INNER), lambda bi, c: (bi, c, SSM_CONV_DIM // SSM_INNER)),
                  pl.BlockSpec((8, SSM_CONV_DIM), lambda bi, c: (0, 0)),
                  pl.BlockSpec((1, SSM_CONV_DIM), lambda bi, c: (0, 0)),
                  pl.BlockSpec((1, LANES), lambda bi, c: (0, 0)),
                  pl.BlockSpec((1, LANES), lambda bi, c: (0, 0)),
                  pl.BlockSpec((1, SSM_INNER), lambda bi, c: (0, 0)),
                  pl.BlockSpec((1, SSM_INNER), lambda bi, c: (0, 0))],
        out_specs=pl.BlockSpec((1, ln, SSM_INNER), lambda bi, c: (bi, c, 0)),
        out_shape=jax.ShapeDtypeStruct((b, s, SSM_INNER), BF16),
        scratch_shapes=[pltpu.VMEM((SSM_GROUPS, SSM_STATE, SSM_INNER // SSM_GROUPS), F32),
                        pltpu.VMEM((ln, SSM_CONV_DIM), F32),
                        pltpu.VMEM((ln, SSM_INNER), F32)],
        compiler_params=_cparams("parallel", "arbitrary"),
        name="ssd",
    )(proj, proj, dt_raw, proj, cw, conv_b.reshape(1, -1),
      jnp.pad(dt_bias, (0, pad)).reshape(1, LANES), jnp.pad(a_log, (0, pad)).reshape(1, LANES),
      jnp.repeat(d_skip, HEAD_DIM).reshape(1, SSM_INNER), ssm_norm.reshape(1, SSM_INNER))


def _rel_bucket(n):
    n = jnp.maximum(n, 0)
    nf = jnp.maximum(n, 1).astype(F32)
    large = REL_MAX_EXACT + (jnp.log(nf / REL_MAX_EXACT) / math.log(REL_MAX_DIST / REL_MAX_EXACT)
                             * (REL_BUCKETS - REL_MAX_EXACT)).astype(jnp.int32)
    large = jnp.minimum(large, REL_BUCKETS - 1)
    return jnp.where(n < REL_MAX_EXACT, n, large)


def _bias_lookup(rb_ref, head, bucket):
    out = jnp.full(bucket.shape, rb_ref[head, 0], F32)
    for kk in range(1, REL_BUCKETS):
        out = jnp.where(bucket >= kk, rb_ref[head, kk], out)
    return out


N_NEAR = 8


def _bias_tiles_kernel(rb_ref, o_ref):
    h = pl.program_id(0)
    row = lax.broadcasted_iota(jnp.int32, (QT, QT), 0)
    col = lax.broadcasted_iota(jnp.int32, (QT, QT), 1)
    for d in range(N_NEAR):
        bucket = _rel_bucket(d * QT + col - row)
        o_ref[0, d] = _bias_lookup(rb_ref, h, bucket) - rb_ref[h, REL_BUCKETS - 1]


def _bias_tiles(rel_bias):
    return pl.pallas_call(
        _bias_tiles_kernel,
        grid=(NSA_HEADS,),
        in_specs=[pl.BlockSpec(memory_space=pltpu.SMEM)],
        out_specs=pl.BlockSpec((1, N_NEAR, QT, QT), lambda h: (h, 0, 0, 0)),
        out_shape=jax.ShapeDtypeStruct((NSA_HEADS, N_NEAR, QT, QT), F32),
        compiler_params=_cparams("parallel"),
        name="nsa_bias_tiles",
    )(rel_bias)


def _compress_kernel(a_ref, w1_ref, b1_ref, w2_ref, pe_ref, o_ref):
    a = a_ref[0, 0, 0]
    ncp = a.shape[0]
    half = CMP_STRIDE * HEAD_DIM
    h1 = _dot(a, w1_ref[0, :half, :])
    h2 = _dot(a, w1_ref[0, half:, :])
    h2 = pltpu.roll(h2, ncp - 1, 0)
    rowi = lax.broadcasted_iota(jnp.int32, h2.shape, 0)
    h2 = jnp.where(rowi < ncp - 1, h2, 0.0)
    pev = _dot(pe_ref[0], w1_ref[0])[0:1]
    hid = h1 + h2 + pev + b1_ref[0]
    act = hid * _sigmoid(hid)
    o_ref[0, 0, 0] = _dot(act.astype(BF16), w2_ref[0]).astype(o_ref.dtype)


def _compress(a, w1, b1, w2, pe):
    _, b, g, ncp, kk = a.shape
    hid = w1.shape[2]
    pe8 = jnp.pad(pe.reshape(2, 1, CMP_BLOCK * HEAD_DIM), ((0, 0), (0, 7), (0, 0))).astype(BF16)
    return pl.pallas_call(
        _compress_kernel,
        grid=(2, b, g),
        in_specs=[pl.BlockSpec((1, 1, 1, ncp, kk), lambda t, bi, gi: (t, bi, gi, 0, 0)),
                  pl.BlockSpec((1, 2 * kk, hid), lambda t, bi, gi: (t, 0, 0)),
                  pl.BlockSpec((1, 1, hid), lambda t, bi, gi: (t, 0, 0)),
                  pl.BlockSpec((1, hid, HEAD_DIM), lambda t, bi, gi: (t, 0, 0)),
                  pl.BlockSpec((1, 8, 2 * kk), lambda t, bi, gi: (t, 0, 0))],
        out_specs=pl.BlockSpec((1, 1, 1, ncp, HEAD_DIM), lambda t, bi, gi: (t, bi, gi, 0, 0)),
        out_shape=jax.ShapeDtypeStruct((2, b, g, ncp, HEAD_DIM), BF16),
        compiler_params=_cparams("parallel", "parallel", "parallel"),
        name="nsa_compress",
    )(a, w1, b1.reshape(2, 1, hid), w2, pe8)


def _cattn_kernel(rb_ref, q_ref, k_ref, v_ref, ov_ref, oc_ref, sel_ref, bias_ref):
    g = pl.program_id(0)
    i = pl.program_id(1)
    ncp = k_ref.shape[2]
    q0 = i * QT
    trow = q0 + lax.broadcasted_iota(jnp.int32, (QT, ncp), 0)
    cend = lax.broadcasted_iota(jnp.int32, (QT, ncp), 1) * CMP_STRIDE + (CMP_BLOCK - 1)
    valid = cend <= trow

    @pl.when(pl.program_id(2) == 0)
    def _():
        bucket = _rel_bucket(trow - cend)
        for h in range(NSA_HG):
            bias_ref[h] = _bias_lookup(rb_ref, g * NSA_HG + h, bucket)

    q4 = q_ref[0, 0].reshape(NSA_HG * QT, HEAD_DIM)
    v = v_ref[0, 0]
    qk = _dot_nt(q4, k_ref[0, 0])
    pcsum = jnp.zeros((QT, ncp), F32)
    for h in range(NSA_HG):
        lg = jnp.where(valid, qk[h * QT:(h + 1) * QT] + bias_ref[h], NEG)
        m = jnp.max(lg, axis=1, keepdims=True)
        e = jnp.where(valid, jnp.exp(lg - m), 0.0)
        ssum = jnp.sum(e, axis=1, keepdims=True)
        p = e / jnp.where(ssum > 0.0, ssum, 1.0)
        pcsum = pcsum + p
        oc_ref[0, 0, h] = _dot(p.astype(BF16), v).astype(oc_ref.dtype)

    hi = pcsum.astype(BF16)
    lo = (pcsum - hi.astype(F32)).astype(BF16)
    imp = _dot_nt(ov_ref[...], hi) + _dot_nt(ov_ref[...], lo)
    jrow = lax.broadcasted_iota(jnp.int32, (LANES, QT), 0)
    tq = q0 + lax.broadcasted_iota(jnp.int32, (LANES, QT), 1)
    cur = jnp.right_shift(tq, SEL_BLOCK.bit_length() - 1)
    forced = (jrow == 0) | (jrow == cur) | (jrow == cur - 1)
    val = jnp.where(forced, FORCE, jnp.where(jrow * SEL_BLOCK <= tq, imp, -1.0))
    jf = jrow.astype(F32)
    sel = jnp.zeros((LANES, QT), F32)
    for _ in range(N_SEL):
        m = jnp.max(val, axis=0, keepdims=True)
        jm = jnp.min(jnp.where(val == m, jf, 1e9), axis=0, keepdims=True)
        hit = jf == jm
        sel = jnp.where(hit, 1.0, sel)
        val = jnp.where(hit, M_INIT, val)
    sel_ref[0, 0, 0] = ((sel - 1.0) * SEL_MASK).astype(sel_ref.dtype)


def _overlap_t(ncp):
    ci = (np.arange(ncp) * CMP_STRIDE)[None, :]
    sj = (np.arange(LANES) * SEL_BLOCK)[:, None]
    return jnp.asarray(((ci < sj + SEL_BLOCK) & (ci + CMP_BLOCK > sj)).astype(np.float32), dtype=BF16)


def _cattn(rel_bias, q, kcmp, vcmp):
    b, g, hg, s, d = q.shape
    ncp = kcmp.shape[2]
    nq = s // QT
    return pl.pallas_call(
        _cattn_kernel,
        grid=(g, nq, b),
        in_specs=[pl.BlockSpec(memory_space=pltpu.SMEM),
                  pl.BlockSpec((1, 1, hg, QT, d), lambda gi, i, bi: (bi, gi, 0, i, 0)),
                  pl.BlockSpec((1, 1, ncp, d), lambda gi, i, bi: (bi, gi, 0, 0)),
                  pl.BlockSpec((1, 1, ncp, d), lambda gi, i, bi: (bi, gi, 0, 0)),
                  pl.BlockSpec((LANES, ncp), lambda gi, i, bi: (0, 0))],
        out_specs=[pl.BlockSpec((1, 1, hg, QT, d), lambda gi, i, bi: (bi, gi, 0, i, 0)),
                   pl.BlockSpec((1, 1, 1, LANES, QT), lambda gi, i, bi: (bi, gi, i, 0, 0))],
        out_shape=[jax.ShapeDtypeStruct((b, g, hg, s, d), BF16),
                   jax.ShapeDtypeStruct((b, g, nq, LANES, QT), BF16)],
        scratch_shapes=[pltpu.VMEM((hg, QT, ncp), F32)],
        compiler_params=_cparams("parallel", "parallel", "arbitrary"),
        name="nsa_cmp_attn",
    )(rel_bias, q, kcmp, vcmp, _overlap_t(ncp))


VROWS = 80


def _sattn_kernel(qt_ref, sel_ref, ks_ref, vs_ref, kw_ref, vw_ref, tt_ref, oc_ref, gt_ref, o_ref,
                  m_ref, acc_ref):
    i = pl.program_id(2)
    cols = NSA_HG * QT
    qt = qt_ref[0, 0, 0]
    qa = jnp.concatenate([qt, jnp.concatenate([sel_ref[0, 0, 0]] * NSA_HG, axis=1)], axis=0)
    krow = lax.broadcasted_iota(jnp.int32, (QT, cols), 0)
    qcol = lax.broadcasted_iota(jnp.int32, (QT, cols), 1) & (QT - 1)

    def reset():
        m_ref[...] = jnp.full(m_ref.shape, M_INIT, F32)
        acc_ref[...] = jnp.zeros(acc_ref.shape, F32)

    def update(s, vt):
        m_prev = m_ref[...]
        m_new = jnp.maximum(m_prev, jnp.max(s, axis=0, keepdims=True))
        alpha = jnp.exp(m_prev - m_new)
        p = jnp.exp(s - m_new[0:1])
        acc_ref[...] = acc_ref[...] * alpha[0:1] + _dot(vt, p.astype(BF16))
        m_ref[...] = m_new

    def near_bias(d):
        return jnp.concatenate([tt_ref[h, d] for h in range(NSA_HG)], axis=1)

    def finish():
        acc = acc_ref[...]
        return acc[:HEAD_DIM] / acc[HEAD_DIM:HEAD_DIM + 1]

    def sel_tile(kt, d, size=QT):
        k0 = pl.multiple_of(kt * QT, QT)
        s = _dot(ks_ref[0, 0, pl.ds(k0, size), :], qa)
        if d is not None:
            s = s + near_bias(d)
        return s, vs_ref[0, 0, :, pl.ds(k0, size)]

    def group(k_ref, v_ref, q_op, kt0, ds, first_gt, last_le):
        n = len(ds)
        k0 = pl.multiple_of(kt0 * QT, QT)
        s = _dot(k_ref[0, 0, pl.ds(k0, n * QT), :], q_op)
        parts = []
        for t, d in enumerate(ds):
            blk = s[t * QT:(t + 1) * QT] + near_bias(d)
            if t == 0 and first_gt:
                blk = jnp.where(krow > qcol, blk, NEG)
            if t == n - 1 and last_le:
                blk = jnp.where(krow <= qcol, blk, NEG)
            parts.append(blk)
        update(jnp.concatenate(parts, axis=0), v_ref[0, 0, :, pl.ds(k0, n * QT)])

    reset()
    n_far = jnp.maximum(i - (N_NEAR - 1), 0)

    def far_body(it, carry):
        update(*sel_tile(4 * it, None, 4 * QT))
        return carry

    def near_body(kt, carry):
        update(*sel_tile(kt, i - kt))
        return carry

    lax.fori_loop(0, n_far // 4, far_body, 0)
    for size in (2, 1):
        @pl.when((n_far & size) != 0)
        def _(size=size):
            update(*sel_tile((n_far // (2 * size)) * (2 * size), None, size * QT))

    half = N_NEAR // 2

    @pl.when(i >= N_NEAR - 1)
    def _():
        group(ks_ref, vs_ref, qa, i - (N_NEAR - 1), list(range(N_NEAR - 1, half - 1, -1)), False, False)
        group(ks_ref, vs_ref, qa, i - (half - 1), list(range(half - 1, -1, -1)), False, True)

    @pl.when(i < N_NEAR - 1)
    def _():
        lax.fori_loop(0, i, near_body, 0)
        s, vt = sel_tile(i, 0)
        update(jnp.where(krow <= qcol, s, NEG), vt)

    osl = finish()

    reset()
    nwin = WINDOW // QT

    @pl.when(i >= nwin)
    def _():
        group(kw_ref, vw_ref, qt, i - nwin, list(range(nwin, -1, -1)), True, True)

    @pl.when(i < nwin)
    def _():
        for d in range(nwin - 1, -1, -1):
            @pl.when(i - d >= 0)
            def _(d=d):
                group(kw_ref, vw_ref, qt, i - d, [d], False, d == 0)

    ow = finish()

    gt = _sigmoid(gt_ref[0, 0])
    gate = lambda br: jnp.concatenate([gt[3 * h + br:3 * h + br + 1] for h in range(NSA_HG)], axis=1)
    o = gate(0) * oc_ref[0, 0, 0].astype(F32) + gate(1) * osl + gate(2) * ow
    o_ref[0, 0, 0] = o.astype(o_ref.dtype)


def _sattn(qt, selb, ksa, vst, kw, vwt, tt, oct, gates_t):
    b, g, nq, d, cols = qt.shape
    s = kw.shape[2]
    hg = NSA_HG
    k_spec = lambda width: pl.BlockSpec((1, 1, s, width), lambda bi, gi, i: (bi, gi, 0, 0))
    v_spec = pl.BlockSpec((1, 1, VROWS, s), lambda bi, gi, i: (bi, gi, 0, 0))
    q_spec = pl.BlockSpec((1, 1, 1, d, cols), lambda bi, gi, i: (bi, gi, i, 0, 0))
    return pl.pallas_call(
        _sattn_kernel,
        grid=(b, g, nq),
        in_specs=[q_spec,
                  pl.BlockSpec((1, 1, 1, LANES, QT), lambda bi, gi, i: (bi, gi, i, 0, 0)),
                  k_spec(ksa.shape[3]), v_spec, k_spec(d), v_spec,
                  pl.BlockSpec((hg, N_NEAR, QT, QT), lambda bi, gi, i: (gi, 0, 0, 0)),
                  q_spec,
                  pl.BlockSpec((1, 1, 16, QT), lambda bi, gi, i: (bi, gi, 0, i))],
        out_specs=q_spec,
        out_shape=jax.ShapeDtypeStruct((b, g, nq, d, cols), BF16),
        scratch_shapes=[pltpu.VMEM((8, cols), F32),
                        pltpu.VMEM((VROWS, cols), F32)],
        compiler_params=_cparams("parallel", "parallel", "arbitrary"),
        name="nsa_sel_win_attn",
    )(qt, selb, ksa, vst, kw, vwt, tt, oct, gates_t)


def _heads(t, h):
    b, s, _ = t.shape
    return t.reshape(b, s, h, HEAD_DIM).transpose(0, 2, 1, 3)


def _mixer_ab(x, g, scale, shift, w_in, conv_w, conv_b, dt_bias, a_log, d_skip, ssm_norm):
    b, s, d = x.shape
    o3 = 3 * SB_WIDTH
    qs = HEAD_DIM ** -0.5
    w_q, w_k, w_v = w_in[:, :SB_WIDTH] * qs, w_in[:, SB_WIDTH:2 * SB_WIDTH], w_in[:, 2 * SB_WIDTH:o3]
    w_z = w_in[:, o3:o3 + SSM_INNER]
    w_xbc = w_in[:, o3 + SSM_INNER:o3 + SSM_INNER + SSM_CONV_DIM]
    w_dt = w_in[:, o3 + SSM_INNER + SSM_CONV_DIM:]
    w_main = jnp.concatenate([w_xbc, w_z, w_q, w_k, w_v], axis=1).astype(BF16)
    w_dt = jnp.pad(w_dt, ((0, 0), (0, LANES - SSM_HEADS))).astype(BF16)
    proj = _norm_proj(x, g, scale, shift, w_main, BF16, 512, "ab_in_proj")
    dt_raw = _norm_proj(x, g, scale, shift, w_dt, F32, LANES, "ab_dt_proj")
    c0 = SSM_CONV_DIM + SSM_INNER
    q = _heads(proj[..., c0:c0 + SB_WIDTH], SB_HEADS)
    k = _heads(proj[..., c0 + SB_WIDTH:c0 + 2 * SB_WIDTH], SB_HEADS)
    v = _heads(proj[..., c0 + 2 * SB_WIDTH:c0 + 3 * SB_WIDTH], SB_HEADS)
    o_sb = _sb_attention(q, k, v).transpose(0, 2, 1, 3).reshape(b, s, SB_WIDTH)
    y = _ssd(proj, dt_raw, conv_w, conv_b, dt_bias, a_log, d_skip, ssm_norm)
    return o_sb, y


def _mixer_nsa(x, g, scale, shift, tt, rel_bias, w_in, cmp_w1, cmp_b1, cmp_w2, cmp_pe):
    b, s, d = x.shape
    gq, hg = NSA_GROUPS, NSA_HG
    qs = HEAD_DIM ** -0.5
    n_main = NSA_Q + 6 * NSA_KV
    w_main = jnp.concatenate([w_in[:, :NSA_Q] * qs, w_in[:, NSA_Q:n_main]], axis=1).astype(BF16)
    n_gate = NSA_HEADS * N_BRANCH
    w_gate = jnp.pad(w_in[:, n_main:], ((0, 0), (0, LANES - n_gate))).astype(BF16)
    proj = _norm_proj(x, g, scale, shift, w_main, BF16, 512, "nsa_in_proj")
    gates = _norm_proj(x, g, scale, shift, w_gate, F32, LANES, "nsa_gate_proj")[..., :n_gate]
    gates_t = gates.reshape(b, s, gq, hg * N_BRANCH).transpose(0, 2, 3, 1)
    gates_t = jnp.pad(gates_t, ((0, 0), (0, 0), (0, 16 - hg * N_BRANCH), (0, 0)))
    nq = s // QT
    q5 = proj[..., :NSA_Q].reshape(b, nq, QT, gq, hg, HEAD_DIM)
    q = q5.transpose(0, 3, 4, 1, 2, 5).reshape(b, gq, hg, s, HEAD_DIM)
    qt = q5.transpose(0, 3, 1, 5, 4, 2).reshape(b, gq, nq, HEAD_DIM, hg * QT)
    part = lambda j: proj[..., NSA_Q + j * NSA_KV:NSA_Q + (j + 1) * NSA_KV]
    nc = s // CMP_STRIDE
    kvc = jnp.stack([part(0), part(1)])
    a = kvc.reshape(2, b, nc, CMP_STRIDE, gq, HEAD_DIM).transpose(0, 1, 4, 2, 3, 5)
    a = a.reshape(2, b, gq, nc, CMP_STRIDE * HEAD_DIM)
    cmp = _compress(a, cmp_w1.astype(BF16), cmp_b1, cmp_w2.astype(BF16), cmp_pe)
    oc, selb = _cattn(rel_bias, q, cmp[0], cmp[1])
    grp = lambda t: t.reshape(b, s, gq, HEAD_DIM).transpose(0, 2, 1, 3)
    onehot = (jnp.arange(s)[:, None] // SEL_BLOCK == jnp.arange(LANES)[None, :]).astype(BF16)
    ksa = jnp.concatenate([grp(part(2)), jnp.broadcast_to(onehot, (b, gq, s, LANES))], axis=-1)

    def values_t(t):
        vt = t.reshape(b, s, gq, HEAD_DIM).transpose(0, 2, 3, 1)
        ones = jnp.ones((b, gq, 1, s), BF16)
        return jnp.concatenate([vt, ones, jnp.zeros((b, gq, VROWS - HEAD_DIM - 1, s), BF16)], axis=2)

    oct = oc.reshape(b, gq, hg, nq, QT, HEAD_DIM).transpose(0, 1, 3, 5, 2, 4).reshape(b, gq, nq, HEAD_DIM, hg * QT)
    ot = _sattn(qt, selb, ksa, values_t(part(3)), grp(part(4)), values_t(part(5)), tt, oct, gates_t)
    o = ot.reshape(b, gq, nq, HEAD_DIM, hg, QT).transpose(0, 2, 5, 1, 4, 3)
    return o.reshape(b, s, NSA_Q)


def kernel(x, c, rel_bias, ada_w, ada_b, norm_g, ab_w_in, ab_conv_w, ab_conv_b, ab_dt_bias, ab_a_log,
           ab_d_skip, ab_ssm_norm, ab_w_out, nsa_w_in, nsa_cmp_w1, nsa_cmp_b1, nsa_cmp_w2, nsa_cmp_pe,
           nsa_w_out, ffn_w_up, ffn_conv_w, ffn_conv_b, ffn_w_down):
    depth = ada_w.shape[0]
    d = x.shape[-1]
    assert x.shape[1] % 1024 == 0 and x.shape[1] // SEL_BLOCK <= LANES
    mod = _modulation(c, ada_w, ada_b)
    tt = _bias_tiles(rel_bias) if depth > 1 else None
    for l in range(depth):
        shift1, scale1, gate1, shift2, scale2, gate2 = [mod[l, :, j * d:(j + 1) * d] for j in range(6)]
        if l % 2 == 0:
            e = l // 2
            o_sb, y = _mixer_ab(x, norm_g[l, 0], scale1, shift1, ab_w_in[e], ab_conv_w[e], ab_conv_b[e],
                                ab_dt_bias[e], ab_a_log[e], ab_d_skip[e], ab_ssm_norm[e])
            w_out = ab_w_out[e].astype(BF16)
            x = _out_proj([o_sb, y], [w_out[:SB_WIDTH], w_out[SB_WIDTH:]], norm_g[l, 1], gate1, x, "ab_out_proj")
        else:
            o = l // 2
            a = _mixer_nsa(x, norm_g[l, 0], scale1, shift1, tt, rel_bias, nsa_w_in[o], nsa_cmp_w1[o],
                           nsa_cmp_b1[o], nsa_cmp_w2[o], nsa_cmp_pe[o])
            x = _out_proj([a], [nsa_w_out[o].astype(BF16)], norm_g[l, 1], gate1, x, "nsa_out_proj")
        u = _norm_proj(x, norm_g[l, 2], scale2, shift2, ffn_w_up[l].astype(BF16), BF16, 512, "ffn_up_proj")
        x = _ffn_down(u, ffn_conv_w[l], ffn_conv_b[l], ffn_w_down[l].astype(BF16), norm_g[l, 3], gate2, x)
    return x
```

```python
import functools
import math

import numpy as np
import jax
import jax.numpy as jnp
from jax import lax
from jax.experimental import pallas as pl
from jax.experimental.pallas import tpu as pltpu

BF16 = jnp.bfloat16
F32 = jnp.float32

SB_HEADS = 8
HEAD_DIM = 64
SB_WIDTH = SB_HEADS * HEAD_DIM
SSM_HEADS = 16
SSM_INNER = SSM_HEADS * HEAD_DIM
SSM_GROUPS = 4
SSM_STATE = 128
SSM_CONV = 4
SSM_CHUNK = 128
SSM_CONV_DIM = SSM_INNER + 2 * SSM_GROUPS * SSM_STATE
NSA_HEADS = 16
NSA_GROUPS = 4
NSA_HG = NSA_HEADS // NSA_GROUPS
NSA_Q = NSA_HEADS * HEAD_DIM
NSA_KV = NSA_GROUPS * HEAD_DIM
CMP_BLOCK = 32
CMP_STRIDE = 16
SEL_BLOCK = 64
N_SEL = 16
WINDOW = 512
N_BRANCH = 3
REL_BUCKETS = 32
REL_MAX_EXACT = 16
REL_MAX_DIST = 1024
FFN_CONV = 3
EPS = 1e-6
NEG = -1e30
FORCE = 1e4
SEL_MASK = 2.0 ** 100
M_INIT = -3.0e38

QT = 128
LANES = 128
VMEM_LIMIT = 48 * 1024 * 1024


def _cparams(*sem):
    return pltpu.CompilerParams(dimension_semantics=sem, vmem_limit_bytes=VMEM_LIMIT)


def _sigmoid(x):
    return 1.0 / (1.0 + jnp.exp(-x))


def _dot(a, b):
    return jnp.dot(a, b, preferred_element_type=F32)


def _dot_nt(a, b):
    return lax.dot_general(a, b, (((1,), (1,)), ((), ())), preferred_element_type=F32)


def _rms(y, g):
    ms = jnp.mean(y * y, axis=-1, keepdims=True)
    return y * lax.rsqrt(ms + EPS) * g


def _mod_kernel(c_ref, w_ref, b_ref, o_ref):
    c = c_ref[...]
    s = c * _sigmoid(c)
    o_ref[0] = _dot(s.astype(BF16), w_ref[0].astype(BF16)) + b_ref[0]


def _modulation(c, ada_w, ada_b):
    depth, d, n = ada_w.shape
    b = c.shape[0]
    bp = 8
    tn = 768
    cp = jnp.pad(c, ((0, bp - b), (0, 0)))
    out = pl.pallas_call(
        _mod_kernel,
        grid=(depth, n // tn),
        in_specs=[pl.BlockSpec((bp, d), lambda l, j: (0, 0)),
                  pl.BlockSpec((1, d, tn), lambda l, j: (l, 0, j)),
                  pl.BlockSpec((1, 1, tn), lambda l, j: (l, 0, j))],
        out_specs=pl.BlockSpec((1, bp, tn), lambda l, j: (l, 0, j)),
        out_shape=jax.ShapeDtypeStruct((depth, bp, n), F32),
        compiler_params=_cparams("parallel", "parallel"),
        name="adaln_mod",
    )(cp, ada_w, ada_b.reshape(depth, 1, n))
    return out[:, :b]


def _norm_proj_kernel(x_ref, g_ref, sc_ref, sh_ref, w_ref, o_ref, *, tn):
    y = _rms(x_ref[0], g_ref[...])
    h = (y * (1.0 + sc_ref[0]) + sh_ref[0]).astype(BF16)
    for j in range(w_ref.shape[1] // tn):
        o_ref[0, :, j * tn:(j + 1) * tn] = _dot(h, w_ref[:, j * tn:(j + 1) * tn]).astype(o_ref.dtype)


def _norm_proj(x, g, scale, shift, w, out_dtype, tn, name):
    b, s, d = x.shape
    n = w.shape[1]
    tm = min(512, s)
    return pl.pallas_call(
        functools.partial(_norm_proj_kernel, tn=tn),
        grid=(b, s // tm),
        in_specs=[pl.BlockSpec((1, tm, d), lambda bi, i: (bi, i, 0)),
                  pl.BlockSpec((1, d), lambda bi, i: (0, 0)),
                  pl.BlockSpec((1, 1, d), lambda bi, i: (bi, 0, 0)),
                  pl.BlockSpec((1, 1, d), lambda bi, i: (bi, 0, 0)),
                  pl.BlockSpec((d, n), lambda bi, i: (0, 0))],
        out_specs=pl.BlockSpec((1, tm, n), lambda bi, i: (bi, i, 0)),
        out_shape=jax.ShapeDtypeStruct((b, s, n), out_dtype),
        compiler_params=_cparams("parallel", "parallel"),
        name=name,
    )(x, g.reshape(1, d), scale.reshape(b, 1, d), shift.reshape(b, 1, d), w)


def _out_proj_kernel(*refs, n_in):
    a_refs = refs[:n_in]
    w_refs = refs[n_in:2 * n_in]
    g_ref, gate_ref, x_ref, o_ref = refs[2 * n_in:]
    acc = _dot(a_refs[0][0], w_refs[0][...])
    for a_ref, w_ref in zip(a_refs[1:], w_refs[1:]):
        acc = acc + _dot(a_ref[0], w_ref[...])
    o_ref[0] = x_ref[0] + gate_ref[0] * _rms(acc, g_ref[...])


def _out_proj(acts, ws, g, gate, x, name):
    b, s, d = x.shape
    tm = min(512, s)
    n_in = len(acts)
    in_specs = [pl.BlockSpec((1, tm, a.shape[2]), lambda bi, i: (bi, i, 0)) for a in acts]
    in_specs += [pl.BlockSpec(w.shape, lambda bi, i: (0, 0)) for w in ws]
    in_specs += [pl.BlockSpec((1, d), lambda bi, i: (0, 0)),
                 pl.BlockSpec((1, 1, d), lambda bi, i: (bi, 0, 0)),
                 pl.BlockSpec((1, tm, d), lambda bi, i: (bi, i, 0))]
    return pl.pallas_call(
        functools.partial(_out_proj_kernel, n_in=n_in),
        grid=(b, s // tm),
        in_specs=in_specs,
        out_specs=pl.BlockSpec((1, tm, d), lambda bi, i: (bi, i, 0)),
        out_shape=jax.ShapeDtypeStruct((b, s, d), F32),
        compiler_params=_cparams("parallel", "parallel"),
        name=name,
    )(*acts, *ws, g.reshape(1, d), gate.reshape(b, 1, d), x)


def _shift_rows(cur, prev, k):
    sc = pltpu.roll(cur, k, 0)
    sp = pltpu.roll(prev, k, 0)
    row8 = lax.broadcasted_iota(jnp.int32, sp.shape, 0)
    head = jnp.where(row8 < k, sp, sc[:8])
    return jnp.concatenate([head, sc[8:]], axis=0)


def _gelu_tanh(x):
    return 0.5 * x * (1.0 + jnp.tanh(math.sqrt(2.0 / math.pi) * (x + 0.044715 * (x * x * x))))


def _ffn_down_kernel(u_ref, up_ref, cw_ref, cb_ref, w_ref, g_ref, gate_ref, x_ref, o_ref, a_ref, *, dff, ck):
    first = pl.program_id(1) == 0

    def conv(col0):
        cur = u_ref[0, :, col0:col0 + ck].astype(F32)
        prev = up_ref[0, :, col0:col0 + ck].astype(F32)
        prev = jnp.where(first, 0.0, prev)
        w = cw_ref[:, col0:col0 + ck]
        y = w[2:3] * cur + cb_ref[:, col0:col0 + ck]
        y = y + w[1:2] * _shift_rows(cur, prev, 1)
        return y + w[0:1] * _shift_rows(cur, prev, 2)

    for c in range(dff // ck):
        act = _gelu_tanh(conv(c * ck)) * conv(dff + c * ck)
        a_ref[:, c * ck:(c + 1) * ck] = act.astype(BF16)
    acc = _dot(a_ref[...], w_ref[...])
    o_ref[0] = x_ref[0] + gate_ref[0] * _rms(acc, g_ref[...])


def _ffn_down(u, conv_w, conv_b, w_down, g, gate, x):
    b, s, d = x.shape
    dff = w_down.shape[0]
    c2 = 2 * dff
    tm = min(256, s)
    cw = jnp.pad(conv_w, ((0, 8 - conv_w.shape[0]), (0, 0)))
    return pl.pallas_call(
        functools.partial(_ffn_down_kernel, dff=dff, ck=256),
        grid=(b, s // tm),
        in_specs=[pl.BlockSpec((1, tm, c2), lambda bi, i: (bi, i, 0)),
                  pl.BlockSpec((1, 8, c2), lambda bi, i: (bi, jnp.maximum(i * (tm // 8) - 1, 0), 0)),
                  pl.BlockSpec((8, c2), lambda bi, i: (0, 0)),
                  pl.BlockSpec((1, c2), lambda bi, i: (0, 0)),
                  pl.BlockSpec((dff, d), lambda bi, i: (0, 0)),
                  pl.BlockSpec((1, d), lambda bi, i: (0, 0)),
                  pl.BlockSpec((1, 1, d), lambda bi, i: (bi, 0, 0)),
                  pl.BlockSpec((1, tm, d), lambda bi, i: (bi, i, 0))],
        out_specs=pl.BlockSpec((1, tm, d), lambda bi, i: (bi, i, 0)),
        out_shape=jax.ShapeDtypeStruct((b, s, d), F32),
        scratch_shapes=[pltpu.VMEM((tm, dff), BF16)],
        compiler_params=_cparams("parallel", "arbitrary"),
        name="ffn_down",
    )(u, u, cw, conv_b.reshape(1, c2), w_down, g.reshape(1, d), gate.reshape(b, 1, d), x)


SBQ = 512


def _sb_kernel(q_ref, k_ref, v_ref, u_ref, o_ref, acc_ref, cs_ref):
    i = pl.program_id(2)
    q = q_ref[0, 0]
    nsub = SBQ // QT
    diff = (lax.broadcasted_iota(jnp.int32, (SBQ, QT), 1) - lax.broadcasted_iota(jnp.int32, (SBQ, QT), 0))
    acc_ref[...] = jnp.zeros(acc_ref.shape, F32)
    cs_ref[...] = jnp.zeros(cs_ref.shape, F32)

    def tiles(kidxs, diag):
        offs = [pl.multiple_of(kidx * QT, QT) for kidx in kidxs]
        zs = [_dot_nt(q, k_ref[0, 0, pl.ds(k0, QT), :]) for k0 in offs]
        lss, css, belows = [], [], []
        for kidx, z in zip(kidxs, zs):
            ls = jnp.minimum(z, 0.0) - jnp.log(1.0 + jnp.exp(-jnp.abs(z)))
            lr = ls - z
            below = None
            if diag:
                below = diff < i * SBQ - kidx * QT
                lr = jnp.where(below, lr, 0.0)
            hi = lr.astype(BF16)
            lo = (lr - hi.astype(F32)).astype(BF16)
            css.append(_dot(jnp.concatenate([hi, lo], axis=1), u_ref[...]))
            lss.append(ls)
            belows.append(below)
        csum = cs_ref[...]
        acc = acc_ref[...]
        for k0, ls, cs, below in zip(offs, lss, css, belows):
            w = jnp.exp(ls + cs[:, :QT] + csum)
            if diag:
                w = jnp.where(below, w, 0.0)
            acc = acc + _dot(w.astype(BF16), v_ref[0, 0, pl.ds(k0, QT), :])
            csum = csum + cs[:, QT:]
        acc_ref[...] = acc
        cs_ref[...] = csum

    tiles([i * nsub + r for r in range(nsub - 1, -1, -1)], True)

    def body(it, carry):
        tiles([(i - it) * nsub - 1 - r for r in range(nsub)], False)
        return carry

    lax.fori_loop(0, i, body, 0)
    o_ref[0, 0] = acc_ref[...].astype(o_ref.dtype)


def _suffix_matrix():
    j = np.arange(2 * QT)[:, None] % QT
    s = np.arange(QT + LANES)[None, :]
    return jnp.asarray(np.where(s < QT, j > s, True).astype(np.float32), dtype=BF16)


def _sb_attention(q, k, v):
    b, h, s, d = q.shape
    return pl.pallas_call(
        _sb_kernel,
        grid=(b, h, s // SBQ),
        in_specs=[pl.BlockSpec((1, 1, SBQ, d), lambda bi, hi, i: (bi, hi, i, 0)),
                  pl.BlockSpec((1, 1, s, d), lambda bi, hi, i: (bi, hi, 0, 0)),
                  pl.BlockSpec((1, 1, s, d), lambda bi, hi, i: (bi, hi, 0, 0)),
                  pl.BlockSpec((2 * QT, QT + LANES), lambda bi, hi, i: (0, 0))],
        out_specs=pl.BlockSpec((1, 1, SBQ, d), lambda bi, hi, i: (bi, hi, i, 0)),
        out_shape=jax.ShapeDtypeStruct((b, h, s, d), BF16),
        scratch_shapes=[pltpu.VMEM((SBQ, d), F32), pltpu.VMEM((SBQ, LANES), F32)],
        compiler_params=_cparams("parallel", "parallel", "arbitrary"),
        name="sb_attention",
    )(q, k, v, _suffix_matrix())


def _pair(f, p, lane_lo):
    return jnp.where(lane_lo, f[:, 2 * p:2 * p + 1], f[:, 2 * p + 1:2 * p + 2])


def _ssd_kernel(xr_ref, xp_ref, dt_ref, z_ref, cw_ref, cb_ref, dtb_ref, alog_ref, dsk_ref, nrm_ref,
                o_ref, st_ref, xc_ref, y_ref):
    ln = SSM_CHUNK
    c = pl.program_id(1)

    @pl.when(c == 0)
    def _():
        st_ref[...] = jnp.zeros(st_ref.shape, F32)

    first = c == 0
    ck = 256
    for j in range(SSM_CONV_DIM // ck):
        cur = xr_ref[0, :, j * ck:(j + 1) * ck].astype(F32)
        prev = xp_ref[0, :, j * ck:(j + 1) * ck].astype(F32)
        prev = jnp.where(first, 0.0, prev)
        w = cw_ref[:, j * ck:(j + 1) * ck]
        y = w[3:4] * cur + cb_ref[:, j * ck:(j + 1) * ck]
        for kk in range(1, SSM_CONV):
            y = y + w[3 - kk:4 - kk] * _shift_rows(cur, prev, kk)
        xc_ref[:, j * ck:(j + 1) * ck] = y * _sigmoid(y)

    dtr = dt_ref[0] + dtb_ref[...]
    dt = jnp.maximum(dtr, 0.0) + jnp.log(1.0 + jnp.exp(-jnp.abs(dtr)))
    a = dt * (-jnp.exp(alog_ref[...]))
    row = lax.broadcasted_iota(jnp.int32, (ln, ln), 0)
    col = lax.broadcasted_iota(jnp.int32, (ln, ln), 1)
    causal = col <= row
    acs = jnp.dot(causal.astype(F32), a, precision=lax.Precision.HIGHEST, preferred_element_type=F32)
    acs_t = acs.T
    eacs = jnp.exp(acs)
    dout = jnp.exp(acs[ln - 1:ln, :] - acs)
    lane_lo = lax.broadcasted_iota(jnp.int32, (ln, LANES), 1) < HEAD_DIM

    hg = SSM_HEADS // SSM_GROUPS
    for g in range(SSM_GROUPS):
        bm = xc_ref[:, SSM_INNER + g * SSM_STATE:SSM_INNER + (g + 1) * SSM_STATE]
        cm = xc_ref[:, SSM_INNER + (SSM_GROUPS + g) * SSM_STATE:SSM_INNER + (SSM_GROUPS + g + 1) * SSM_STATE]
        cmb = cm.astype(BF16)
        cb = _dot_nt(cmb, bm.astype(BF16))
        prev = st_ref[g]
        yoff = _dot(cmb, prev.astype(BF16))
        dox_parts = []
        cdec_parts = []
        for pp in range(hg // 2):
            p = g * (hg // 2) + pp
            xs = xc_ref[:, p * LANES:(p + 1) * LANES]
            xdt = xs * _pair(dt, p, lane_lo)
            ms = []
            for hh in (2 * p, 2 * p + 1):
                seg = acs[:, hh:hh + 1] - acs_t[hh:hh + 1, :]
                decay = jnp.exp(jnp.where(causal, seg, NEG))
                ms.append((cb * decay).astype(BF16))
            rhs = jnp.concatenate([jnp.where(lane_lo, xdt, 0.0), jnp.where(lane_lo, 0.0, xdt)], axis=0)
            ydiag = _dot(jnp.concatenate(ms, axis=1), rhs.astype(BF16))
            yo = yoff[:, pp * LANES:(pp + 1) * LANES] * _pair(eacs, p, lane_lo)
            y_ref[:, p * LANES:(p + 1) * LANES] = ydiag + yo + xs * dsk_ref[:, p * LANES:(p + 1) * LANES]
            dox_parts.append((xdt * _pair(dout, p, lane_lo)).astype(BF16))
            cdec_parts.append(_pair(eacs[ln - 1:ln, :], p, lane_lo[0:1]))
        states = _dot(bm.T.astype(BF16), jnp.concatenate(dox_parts, axis=1))
        st_ref[g] = prev * jnp.concatenate(cdec_parts, axis=1) + states

    zf = z_ref[0].astype(F32)
    yz = y_ref[...] * (zf * _sigmoid(zf))
    o_ref[0] = _rms(yz, nrm_ref[...]).astype(o_ref.dtype)


def _ssd(proj, dt_raw, conv_w, conv_b, dt_bias, a_log, d_skip, ssm_norm):
    b, s, _ = proj.shape
    ln = SSM_CHUNK
    cw = jnp.pad(conv_w, ((0, 8 - SSM_CONV), (0, 0)))
    pad = LANES - SSM_HEADS
    return pl.pallas_call(
        _ssd_kernel,
        grid=(b, s // ln),
        in_specs=[pl.BlockSpec((1, ln, SSM_CONV_DIM), lambda bi, c: (bi, c, 0)),
                  pl.BlockSpec((1, 8, SSM_CONV_DIM), lambda bi, c: (bi, jnp.maximum(c * (ln // 8) - 1, 0), 0)),
                  pl.BlockSpec((1, ln, LANES), lambda bi, c: (bi, c, 0)),
                  pl.BlockSpec((1, ln, SSM_INNER), lambda bi, c: (bi, c, SSM_CONV_DIM // SSM_INNER)),
                  pl.BlockSpec((8, SSM_CONV_DIM), lambda bi, c: (0, 0)),
                  pl.BlockSpec((1, SSM_CONV_DIM), lambda bi, c: (0, 0)),
                  pl.BlockSpec((1, LANES), lambda bi, c: (0, 0)),
                  pl.BlockSpec((1, LANES), lambda bi, c: (0, 0)),
                  pl.BlockSpec((1, SSM_INNER), lambda bi, c: (0, 0)),
                  pl.BlockSpec((1, SSM_INNER), lambda bi, c: (0, 0))],
        out_specs=pl.BlockSpec((1, ln, SSM_INNER), lambda bi, c: (bi, c, 0)),
        out_shape=jax.ShapeDtypeStruct((b, s, SSM_INNER), BF16),
        scratch_shapes=[pltpu.VMEM((SSM_GROUPS, SSM_STATE, SSM_INNER // SSM_GROUPS), F32),
                        pltpu.VMEM((ln, SSM_CONV_DIM), F32),
                        pltpu.VMEM((ln, SSM_INNER), F32)],
        compiler_params=_cparams("parallel", "arbitrary"),
        name="ssd",
    )(proj, proj, dt_raw, proj, cw, conv_b.reshape(1, -1),
      jnp.pad(dt_bias, (0, pad)).reshape(1, LANES), jnp.pad(a_log, (0, pad)).reshape(1, LANES),
      jnp.repeat(d_skip, HEAD_DIM).reshape(1, SSM_INNER), ssm_norm.reshape(1, SSM_INNER))


def _rel_bucket(n):
    n = jnp.maximum(n, 0)
    nf = jnp.maximum(n, 1).astype(F32)
    large = REL_MAX_EXACT + (jnp.log(nf / REL_MAX_EXACT) / math.log(REL_MAX_DIST / REL_MAX_EXACT)
                             * (REL_BUCKETS - REL_MAX_EXACT)).astype(jnp.int32)
    large = jnp.minimum(large, REL_BUCKETS - 1)
    return jnp.where(n < REL_MAX_EXACT, n, large)


def _bias_lookup(rb_ref, head, bucket):
    out = jnp.full(bucket.shape, rb_ref[head, 0], F32)
    for kk in range(1, REL_BUCKETS):
        out = jnp.where(bucket >= kk, rb_ref[head, kk], out)
    return out


N_NEAR = 8


def _bias_tiles_kernel(rb_ref, o_ref):
    h = pl.program_id(0)
    row = lax.broadcasted_iota(jnp.int32, (QT, QT), 0)
    col = lax.broadcasted_iota(jnp.int32, (QT, QT), 1)
    for d in range(N_NEAR):
        bucket = _rel_bucket(d * QT + col - row)
        o_ref[0, d] = _bias_lookup(rb_ref, h, bucket) - rb_ref[h, REL_BUCKETS - 1]


def _bias_tiles(rel_bias):
    return pl.pallas_call(
        _bias_tiles_kernel,
        grid=(NSA_HEADS,),
        in_specs=[pl.BlockSpec(memory_space=pltpu.SMEM)],
        out_specs=pl.BlockSpec((1, N_NEAR, QT, QT), lambda h: (h, 0, 0, 0)),
        out_shape=jax.ShapeDtypeStruct((NSA_HEADS, N_NEAR, QT, QT), F32),
        compiler_params=_cparams("parallel"),
        name="nsa_bias_tiles",
    )(rel_bias)


def _compress_kernel(a_ref, w1_ref, b1_ref, w2_ref, pe_ref, o_ref):
    a = a_ref[0, 0, 0]
    ncp = a.shape[0]
    half = CMP_STRIDE * HEAD_DIM
    h1 = _dot(a, w1_ref[0, :half, :])
    h2 = _dot(a, w1_ref[0, half:, :])
    h2 = pltpu.roll(h2, ncp - 1, 0)
    rowi = lax.broadcasted_iota(jnp.int32, h2.shape, 0)
    h2 = jnp.where(rowi < ncp - 1, h2, 0.0)
    pev = _dot(pe_ref[0], w1_ref[0])[0:1]
    hid = h1 + h2 + pev + b1_ref[0]
    act = hid * _sigmoid(hid)
    o_ref[0, 0, 0] = _dot(act.astype(BF16), w2_ref[0]).astype(o_ref.dtype)


def _compress(a, w1, b1, w2, pe):
    _, b, g, ncp, kk = a.shape
    hid = w1.shape[2]
    pe8 = jnp.pad(pe.reshape(2, 1, CMP_BLOCK * HEAD_DIM), ((0, 0), (0, 7), (0, 0))).astype(BF16)
    return pl.pallas_call(
        _compress_kernel,
        grid=(2, b, g),
        in_specs=[pl.BlockSpec((1, 1, 1, ncp, kk), lambda t, bi, gi: (t, bi, gi, 0, 0)),
                  pl.BlockSpec((1, 2 * kk, hid), lambda t, bi, gi: (t, 0, 0)),
                  pl.BlockSpec((1, 1, hid), lambda t, bi, gi: (t, 0, 0)),
                  pl.BlockSpec((1, hid, HEAD_DIM), lambda t, bi, gi: (t, 0, 0)),
                  pl.BlockSpec((1, 8, 2 * kk), lambda t, bi, gi: (t, 0, 0))],
        out_specs=pl.BlockSpec((1, 1, 1, ncp, HEAD_DIM), lambda t, bi, gi: (t, bi, gi, 0, 0)),
        out_shape=jax.ShapeDtypeStruct((2, b, g, ncp, HEAD_DIM), BF16),
        compiler_params=_cparams("parallel", "parallel", "parallel"),
        name="nsa_compress",
    )(a, w1, b1.reshape(2, 1, hid), w2, pe8)


BUCKET_CONST_FROM = 800


def _cattn_kernel(rb_ref, qt_ref, k_ref, vt_ref, ov_ref, oc_ref, sel_ref, bias_ref):
    g = pl.program_id(0)
    i = pl.program_id(1)
    nb = k_ref.shape[0]
    ncp = k_ref.shape[2]
    q0 = i * QT
    for r in range(ncp // QT):
        rows = slice(r * QT, (r + 1) * QT)
        min_rel = q0 - (CMP_STRIDE * (r * QT + QT - 1) + CMP_BLOCK - 1)
        max_rel = q0 + QT - 1 - (CMP_STRIDE * r * QT + CMP_BLOCK - 1)

        @pl.when((min_rel < BUCKET_CONST_FROM) & (max_rel >= 0))
        def _(r=r, rows=rows):
            t_r = q0 + lax.broadcasted_iota(jnp.int32, (QT, QT), 1)
            cend_r = (r * QT + lax.broadcasted_iota(jnp.int32, (QT, QT), 0)) * CMP_STRIDE + (CMP_BLOCK - 1)
            bucket = _rel_bucket(t_r - cend_r)
            for h in range(NSA_HG):
                bias_ref[h, rows, :] = jnp.where(cend_r <= t_r, _bias_lookup(rb_ref, g * NSA_HG + h, bucket), NEG)

        @pl.when(min_rel >= BUCKET_CONST_FROM)
        def _(rows=rows):
            for h in range(NSA_HG):
                bias_ref[h, rows, :] = jnp.full((QT, QT), rb_ref[g * NSA_HG + h, REL_BUCKETS - 1], F32)

        @pl.when(max_rel < 0)
        def _(rows=rows):
            for h in range(NSA_HG):
                bias_ref[h, rows, :] = jnp.full((QT, QT), NEG, F32)

    anyvalid = q0 + lax.broadcasted_iota(jnp.int32, (1, QT), 1) >= CMP_BLOCK - 1
    qks = [_dot(k_ref[bi, 0], qt_ref[bi, 0, 0]) for bi in range(nb)]
    pcsums = []
    for bi in range(nb):
        vt = vt_ref[bi, 0]
        pcsum = jnp.zeros((ncp, QT), F32)
        ocs = []
        for h in range(NSA_HG):
            lg = qks[bi][:, h * QT:(h + 1) * QT] + bias_ref[h]
            m = jnp.max(lg, axis=0, keepdims=True)
            e = jnp.exp(lg - m)
            p = e * jnp.where(anyvalid, 1.0 / jnp.sum(e, axis=0, keepdims=True), 0.0)
            pcsum = pcsum + p
            ocs.append(_dot(vt, p.astype(BF16)))
        oc_ref[bi, 0, 0] = jnp.concatenate(ocs, axis=1).astype(oc_ref.dtype)
        pcsums.append(pcsum)

    imps = []
    for pcsum in pcsums:
        hi = pcsum.astype(BF16)
        lo = (pcsum - hi.astype(F32)).astype(BF16)
        imps.append(_dot(ov_ref[...], hi) + _dot(ov_ref[...], lo))
    jrow = lax.broadcasted_iota(jnp.int32, (LANES, QT), 0)
    tq = q0 + lax.broadcasted_iota(jnp.int32, (LANES, QT), 1)
    cur = jnp.right_shift(tq, SEL_BLOCK.bit_length() - 1)
    forced = (jrow == 0) | (jrow == cur) | (jrow == cur - 1)
    jf = jrow.astype(F32)
    for bi, imp in enumerate(imps):
        val = jnp.where(forced, FORCE, jnp.where(jrow * SEL_BLOCK <= tq, imp, -1.0))
        sel = jnp.zeros((LANES, QT), F32)
        for _ in range(N_SEL):
            m = jnp.max(val, axis=0, keepdims=True)
            jm = jnp.min(jnp.where(val == m, jf, 1e9), axis=0, keepdims=True)
            hit = jf == jm
            sel = jnp.where(hit, 1.0, sel)
            val = jnp.where(hit, M_INIT, val)
        sel_ref[bi, 0, 0] = ((sel - 1.0) * SEL_MASK).astype(sel_ref.dtype)


def _overlap_t(ncp):
    ci = (np.arange(ncp) * CMP_STRIDE)[None, :]
    sj = (np.arange(LANES) * SEL_BLOCK)[:, None]
    return jnp.asarray(((ci < sj + SEL_BLOCK) & (ci + CMP_BLOCK > sj)).astype(np.float32), dtype=BF16)


def _cattn(rel_bias, qt, kcmp, vcmp_t):
    b, g, nq, d, cols = qt.shape
    ncp = kcmp.shape[2]
    hg = NSA_HG
    q_spec = pl.BlockSpec((b, 1, 1, d, cols), lambda gi, i: (0, gi, i, 0, 0))
    return pl.pallas_call(
        _cattn_kernel,
        grid=(g, nq),
        in_specs=[pl.BlockSpec(memory_space=pltpu.SMEM),
                  q_spec,
                  pl.BlockSpec((b, 1, ncp, d), lambda gi, i: (0, gi, 0, 0)),
                  pl.BlockSpec((b, 1, d, ncp), lambda gi, i: (0, gi, 0, 0)),
                  pl.BlockSpec((LANES, ncp), lambda gi, i: (0, 0))],
        out_specs=[q_spec,
                   pl.BlockSpec((b, 1, 1, LANES, QT), lambda gi, i: (0, gi, i, 0, 0))],
        out_shape=[jax.ShapeDtypeStruct((b, g, nq, d, cols), BF16),
                   jax.ShapeDtypeStruct((b, g, nq, LANES, QT), BF16)],
        scratch_shapes=[pltpu.VMEM((hg, ncp, QT), F32)],
        compiler_params=_cparams("parallel", "parallel"),
        name="nsa_cmp_attn",
    )(rel_bias, qt, kcmp, vcmp_t, _overlap_t(ncp))


VROWS = 80


def _sattn_kernel(qt_ref, sel_ref, ks_ref, vs_ref, kw_ref, vw_ref, tt_ref, oc_ref, gt_ref, o_ref,
                  m_ref, acc_ref):
    i = pl.program_id(2)
    cols = NSA_HG * QT
    qt = qt_ref[0, 0, 0]
    qa = jnp.concatenate([qt, jnp.concatenate([sel_ref[0, 0, 0]] * NSA_HG, axis=1)], axis=0)
    krow = lax.broadcasted_iota(jnp.int32, (QT, cols), 0)
    qcol = lax.broadcasted_iota(jnp.int32, (QT, cols), 1) & (QT - 1)

    def reset():
        m_ref[...] = jnp.full(m_ref.shape, M_INIT, F32)
        acc_ref[...] = jnp.zeros(acc_ref.shape, F32)

    def update(*tiles):
        m = m_ref[...]
        acc = acc_ref[...]
        for s, vt in tiles:
            m_new = jnp.maximum(m, jnp.max(s, axis=0, keepdims=True))
            p = jnp.exp(s - m_new[0:1])
            acc = acc * jnp.exp(m - m_new)[0:1] + _dot(vt, p.astype(BF16))
            m = m_new
        acc_ref[...] = acc
        m_ref[...] = m

    def near_bias(d):
        return jnp.concatenate([tt_ref[h, d] for h in range(NSA_HG)], axis=1)

    def finish():
        acc = acc_ref[...]
        return acc[:HEAD_DIM] / acc[HEAD_DIM:HEAD_DIM + 1]

    def sel_tile(kt, d, size=QT):
        k0 = pl.multiple_of(kt * QT, QT)
        s = _dot(ks_ref[0, 0, pl.ds(k0, size), :], qa)
        if d is not None:
            s = s + near_bias(d)
        return s, vs_ref[0, 0, :, pl.ds(k0, size)]

    def group(k_ref, v_ref, q_op, kt0, ds, first_gt, last_le):
        n = len(ds)
        k0 = pl.multiple_of(kt0 * QT, QT)
        s = _dot(k_ref[0, 0, pl.ds(k0, n * QT), :], q_op)
        parts = []
        for t, d in enumerate(ds):
            blk = s[t * QT:(t + 1) * QT] + near_bias(d)
            if t == 0 and first_gt:
                blk = jnp.where(krow > qcol, blk, NEG)
            if t == n - 1 and last_le:
                blk = jnp.where(krow <= qcol, blk, NEG)
            parts.append(blk)
        return jnp.concatenate(parts, axis=0), v_ref[0, 0, :, pl.ds(k0, n * QT)]

    reset()
    n_far = jnp.maximum(i - (N_NEAR - 1), 0)

    def far_body(it, carry):
        update(sel_tile(8 * it, None, 4 * QT), sel_tile(8 * it + 4, None, 4 * QT))
        return carry

    def near_body(kt, carry):
        update(sel_tile(kt, i - kt))
        return carry

    lax.fori_loop(0, n_far // 8, far_body, 0)
    for size in (4, 2, 1):
        @pl.when((n_far & size) != 0)
        def _(size=size):
            update(sel_tile((n_far // (2 * size)) * (2 * size), None, size * QT))

    half = N_NEAR // 2

    @pl.when(i >= N_NEAR - 1)
    def _():
        update(group(ks_ref, vs_ref, qa, i - (N_NEAR - 1), list(range(N_NEAR - 1, half - 1, -1)), False, False),
               group(ks_ref, vs_ref, qa, i - (half - 1), list(range(half - 1, -1, -1)), False, True))

    @pl.when(i < N_NEAR - 1)
    def _():
        lax.fori_loop(0, i, near_body, 0)
        s, vt = sel_tile(i, 0)
        update((jnp.where(krow <= qcol, s, NEG), vt))

    osl = finish()

    reset()
    nwin = WINDOW // QT

    @pl.when(i >= nwin)
    def _():
        update(group(kw_ref, vw_ref, qt, i - nwin, list(range(nwin, -1, -1)), True, True))

    @pl.when(i < nwin)
    def _():
        for d in range(nwin - 1, -1, -1):
            @pl.when(i - d >= 0)
            def _(d=d):
                update(group(kw_ref, vw_ref, qt, i - d, [d], False, d == 0))

    ow = finish()

    gt = _sigmoid(gt_ref[0, 0])
    gate = lambda br: jnp.concatenate([gt[3 * h + br:3 * h + br + 1] for h in range(NSA_HG)], axis=1)
    o = gate(0) * oc_ref[0, 0, 0].astype(F32) + gate(1) * osl + gate(2) * ow
    o_ref[0, 0, 0] = o.astype(o_ref.dtype)


def _sattn(qt, selb, ksa, vst, kw, vwt, tt, oct, gates_t):
    b, g, nq, d, cols = qt.shape
    s = kw.shape[2]
    hg = NSA_HG
    k_spec = lambda width: pl.BlockSpec((1, 1, s, width), lambda bi, gi, i: (bi, gi, 0, 0))
    v_spec = pl.BlockSpec((1, 1, VROWS, s), lambda bi, gi, i: (bi, gi, 0, 0))
    q_spec = pl.BlockSpec((1, 1, 1, d, cols), lambda bi, gi, i: (bi, gi, i, 0, 0))
    return pl.pallas_call(
        _sattn_kernel,
        grid=(b, g, nq),
        in_specs=[q_spec,
                  pl.BlockSpec((1, 1, 1, LANES, QT), lambda bi, gi, i: (bi, gi, i, 0, 0)),
                  k_spec(ksa.shape[3]), v_spec, k_spec(d), v_spec,
                  pl.BlockSpec((hg, N_NEAR, QT, QT), lambda bi, gi, i: (gi, 0, 0, 0)),
                  q_spec,
                  pl.BlockSpec((1, 1, 16, QT), lambda bi, gi, i: (bi, gi, 0, i))],
        out_specs=q_spec,
        out_shape=jax.ShapeDtypeStruct((b, g, nq, d, cols), BF16),
        scratch_shapes=[pltpu.VMEM((8, cols), F32),
                        pltpu.VMEM((VROWS, cols), F32)],
        compiler_params=_cparams("parallel", "parallel", "arbitrary"),
        name="nsa_sel_win_attn",
    )(qt, selb, ksa, vst, kw, vwt, tt, oct, gates_t)


def _heads(t, h):
    b, s, _ = t.shape
    return t.reshape(b, s, h, HEAD_DIM).transpose(0, 2, 1, 3)


def _mixer_ab(x, g, scale, shift, w_in, conv_w, conv_b, dt_bias, a_log, d_skip, ssm_norm):
    b, s, d = x.shape
    o3 = 3 * SB_WIDTH
    qs = HEAD_DIM ** -0.5
    w_q, w_k, w_v = w_in[:, :SB_WIDTH] * qs, w_in[:, SB_WIDTH:2 * SB_WIDTH], w_in[:, 2 * SB_WIDTH:o3]
    w_z = w_in[:, o3:o3 + SSM_INNER]
    w_xbc = w_in[:, o3 + SSM_INNER:o3 + SSM_INNER + SSM_CONV_DIM]
    w_dt = w_in[:, o3 + SSM_INNER + SSM_CONV_DIM:]
    w_main = jnp.concatenate([w_xbc, w_z, w_q, w_k, w_v], axis=1).astype(BF16)
    w_dt = jnp.pad(w_dt, ((0, 0), (0, LANES - SSM_HEADS))).astype(BF16)
    proj = _norm_proj(x, g, scale, shift, w_main, BF16, 512, "ab_in_proj")
    dt_raw = _norm_proj(x, g, scale, shift, w_dt, F32, LANES, "ab_dt_proj")
    c0 = SSM_CONV_DIM + SSM_INNER
    q = _heads(proj[..., c0:c0 + SB_WIDTH], SB_HEADS)
    k = _heads(proj[..., c0 + SB_WIDTH:c0 + 2 * SB_WIDTH], SB_HEADS)
    v = _heads(proj[..., c0 + 2 * SB_WIDTH:c0 + 3 * SB_WIDTH], SB_HEADS)
    o_sb = _sb_attention(q, k, v).transpose(0, 2, 1, 3).reshape(b, s, SB_WIDTH)
    y = _ssd(proj, dt_raw, conv_w, conv_b, dt_bias, a_log, d_skip, ssm_norm)
    return o_sb, y


def _mixer_nsa(x, g, scale, shift, tt, rel_bias, w_in, cmp_w1, cmp_b1, cmp_w2, cmp_pe):
    b, s, d = x.shape
    gq, hg = NSA_GROUPS, NSA_HG
    qs = HEAD_DIM ** -0.5
    n_main = NSA_Q + 6 * NSA_KV
    w_main = jnp.concatenate([w_in[:, :NSA_Q] * qs, w_in[:, NSA_Q:n_main]], axis=1).astype(BF16)
    n_gate = NSA_HEADS * N_BRANCH
    w_gate = jnp.pad(w_in[:, n_main:], ((0, 0), (0, LANES - n_gate))).astype(BF16)
    proj = _norm_proj(x, g, scale, shift, w_main, BF16, 512, "nsa_in_proj")
    gates = _norm_proj(x, g, scale, shift, w_gate, F32, LANES, "nsa_gate_proj")[..., :n_gate]
    gates_t = gates.reshape(b, s, gq, hg * N_BRANCH).transpose(0, 2, 3, 1)
    gates_t = jnp.pad(gates_t, ((0, 0), (0, 0), (0, 16 - hg * N_BRANCH), (0, 0)))
    nq = s // QT
    q5 = proj[..., :NSA_Q].reshape(b, nq, QT, gq, hg, HEAD_DIM)
    qt = q5.transpose(0, 3, 1, 5, 4, 2).reshape(b, gq, nq, HEAD_DIM, hg * QT)
    part = lambda j: proj[..., NSA_Q + j * NSA_KV:NSA_Q + (j + 1) * NSA_KV]
    nc = s // CMP_STRIDE
    kvc = jnp.stack([part(0), part(1)])
    a = kvc.reshape(2, b, nc, CMP_STRIDE, gq, HEAD_DIM).transpose(0, 1, 4, 2, 3, 5)
    a = a.reshape(2, b, gq, nc, CMP_STRIDE * HEAD_DIM)
    cmp = _compress(a, cmp_w1.astype(BF16), cmp_b1, cmp_w2.astype(BF16), cmp_pe)
    oct, selb = _cattn(rel_bias, qt, cmp[0], cmp[1].transpose(0, 1, 3, 2))
    grp = lambda t: t.reshape(b, s, gq, HEAD_DIM).transpose(0, 2, 1, 3)
    onehot = (jnp.arange(s)[:, None] // SEL_BLOCK == jnp.arange(LANES)[None, :]).astype(BF16)
    ksa = jnp.concatenate([grp(part(2)), jnp.broadcast_to(onehot, (b, gq, s, LANES))], axis=-1)

    def values_t(t):
        vt = t.reshape(b, s, gq, HEAD_DIM).transpose(0, 2, 3, 1)
        ones = jnp.ones((b, gq, 1, s), BF16)
        return jnp.concatenate([vt, ones, jnp.zeros((b, gq, VROWS - HEAD_DIM - 1, s), BF16)], axis=2)

    ot = _sattn(qt, selb, ksa, values_t(part(3)), grp(part(4)), values_t(part(5)), tt, oct, gates_t)
    o = ot.reshape(b, gq, nq, HEAD_DIM, hg, QT).transpose(0, 2, 5, 1, 4, 3)
    return o.reshape(b, s, NSA_Q)


def kernel(x, c, rel_bias, ada_w, ada_b, norm_g, ab_w_in, ab_conv_w, ab_conv_b, ab_dt_bias, ab_a_log,
           ab_d_skip, ab_ssm_norm, ab_w_out, nsa_w_in, nsa_cmp_w1, nsa_cmp_b1, nsa_cmp_w2, nsa_cmp_pe,
           nsa_w_out, ffn_w_up, ffn_conv_w, ffn_conv_b, ffn_w_down):
    depth = ada_w.shape[0]
    d = x.shape[-1]
    assert x.shape[1] % 1024 == 0 and x.shape[1] // SEL_BLOCK <= LANES
    mod = _modulation(c, ada_w, ada_b)
    tt = _bias_tiles(rel_bias) if depth > 1 else None
    for l in range(depth):
        shift1, scale1, gate1, shift2, scale2, gate2 = [mod[l, :, j * d:(j + 1) * d] for j in range(6)]
        if l % 2 == 0:
            e = l // 2
            o_sb, y = _mixer_ab(x, norm_g[l, 0], scale1, shift1, ab_w_in[e], ab_conv_w[e], ab_conv_b[e],
                                ab_dt_bias[e], ab_a_log[e], ab_d_skip[e], ab_ssm_norm[e])
            w_out = ab_w_out[e].astype(BF16)
            x = _out_proj([o_sb, y], [w_out[:SB_WIDTH], w_out[SB_WIDTH:]], norm_g[l, 1], gate1, x, "ab_out_proj")
        else:
            o = l // 2
            a = _mixer_nsa(x, norm_g[l, 0], scale1, shift1, tt, rel_bias, nsa_w_in[o], nsa_cmp_w1[o],
                           nsa_cmp_b1[o], nsa_cmp_w2[o], nsa_cmp_pe[o])
            x = _out_proj([a], [nsa_w_out[o].astype(BF16)], norm_g[l, 1], gate1, x, "nsa_out_proj")
        u = _norm_proj(x, norm_g[l, 2], scale2, shift2, ffn_w_up[l].astype(BF16), BF16, 512, "ffn_up_proj")
        x = _ffn_down(u, ffn_conv_w[l], ffn_conv_b[l], ffn_w_down[l].astype(BF16), norm_g[l, 3], gate2, x)
    return x
```

```python
import functools
import math

import numpy as np
import jax
import jax.numpy as jnp
from jax import lax
from jax.experimental import pallas as pl
from jax.experimental.pallas import tpu as pltpu

BF16 = jnp.bfloat16
F32 = jnp.float32

SB_HEADS = 8
HEAD_DIM = 64
SB_WIDTH = SB_HEADS * HEAD_DIM
SSM_HEADS = 16
SSM_INNER = SSM_HEADS * HEAD_DIM
SSM_GROUPS = 4
SSM_STATE = 128
SSM_CONV = 4
SSM_CHUNK = 128
SSM_CONV_DIM = SSM_INNER + 2 * SSM_GROUPS * SSM_STATE
NSA_HEADS = 16
NSA_GROUPS = 4
NSA_HG = NSA_HEADS // NSA_GROUPS
NSA_Q = NSA_HEADS * HEAD_DIM
NSA_KV = NSA_GROUPS * HEAD_DIM
CMP_BLOCK = 32
CMP_STRIDE = 16
SEL_BLOCK = 64
N_SEL = 16
WINDOW = 512
N_BRANCH = 3
REL_BUCKETS = 32
REL_MAX_EXACT = 16
REL_MAX_DIST = 1024
FFN_CONV = 3
EPS = 1e-6
NEG = -1e30
FORCE = 1e4
SEL_MASK = 2.0 ** 100
M_INIT = -3.0e38

QT = 128
LANES = 128
VMEM_LIMIT = 48 * 1024 * 1024


def _cparams(*sem):
    return pltpu.CompilerParams(dimension_semantics=sem, vmem_limit_bytes=VMEM_LIMIT)


def _sigmoid(x):
    return 1.0 / (1.0 + jnp.exp(-x))


def _dot(a, b):
    return jnp.dot(a, b, preferred_element_type=F32)


def _dot_nt(a, b):
    return lax.dot_general(a, b, (((1,), (1,)), ((), ())), preferred_element_type=F32)


def _rms(y, g):
    ms = jnp.mean(y * y, axis=-1, keepdims=True)
    return y * lax.rsqrt(ms + EPS) * g


def _mod_kernel(c_ref, w_ref, b_ref, o_ref):
    c = c_ref[...]
    s = c * _sigmoid(c)
    o_ref[0] = _dot(s.astype(BF16), w_ref[0].astype(BF16)) + b_ref[0]


def _modulation(c, ada_w, ada_b):
    depth, d, n = ada_w.shape
    b = c.shape[0]
    bp = 8
    tn = 768
    cp = jnp.pad(c, ((0, bp - b), (0, 0)))
    out = pl.pallas_call(
        _mod_kernel,
        grid=(depth, n // tn),
        in_specs=[pl.BlockSpec((bp, d), lambda l, j: (0, 0)),
                  pl.BlockSpec((1, d, tn), lambda l, j: (l, 0, j)),
                  pl.BlockSpec((1, 1, tn), lambda l, j: (l, 0, j))],
        out_specs=pl.BlockSpec((1, bp, tn), lambda l, j: (l, 0, j)),
        out_shape=jax.ShapeDtypeStruct((depth, bp, n), F32),
        compiler_params=_cparams("parallel", "parallel"),
        name="adaln_mod",
    )(cp, ada_w, ada_b.reshape(depth, 1, n))
    return out[:, :b]


def _norm_proj_kernel(x_ref, g_ref, sc_ref, sh_ref, w_ref, o_ref, *, tn):
    y = _rms(x_ref[0], g_ref[...])
    h = (y * (1.0 + sc_ref[0]) + sh_ref[0]).astype(BF16)
    for j in range(w_ref.shape[1] // tn):
        o_ref[0, :, j * tn:(j + 1) * tn] = _dot(h, w_ref[:, j * tn:(j + 1) * tn]).astype(o_ref.dtype)


def _norm_proj(x, g, scale, shift, w, out_dtype, tn, name):
    b, s, d = x.shape
    n = w.shape[1]
    tm = min(512, s)
    return pl.pallas_call(
        functools.partial(_norm_proj_kernel, tn=tn),
        grid=(b, s // tm),
        in_specs=[pl.BlockSpec((1, tm, d), lambda bi, i: (bi, i, 0)),
                  pl.BlockSpec((1, d), lambda bi, i: (0, 0)),
                  pl.BlockSpec((1, 1, d), lambda bi, i: (bi, 0, 0)),
                  pl.BlockSpec((1, 1, d), lambda bi, i: (bi, 0, 0)),
                  pl.BlockSpec((d, n), lambda bi, i: (0, 0))],
        out_specs=pl.BlockSpec((1, tm, n), lambda bi, i: (bi, i, 0)),
        out_shape=jax.ShapeDtypeStruct((b, s, n), out_dtype),
        compiler_params=_cparams("parallel", "parallel"),
        name=name,
    )(x, g.reshape(1, d), scale.reshape(b, 1, d), shift.reshape(b, 1, d), w)


def _out_proj_kernel(*refs, n_in):
    a_refs = refs[:n_in]
    w_refs = refs[n_in:2 * n_in]
    g_ref, gate_ref, x_ref, o_ref = refs[2 * n_in:]
    acc = _dot(a_refs[0][0], w_refs[0][...])
    for a_ref, w_ref in zip(a_refs[1:], w_refs[1:]):
        acc = acc + _dot(a_ref[0], w_ref[...])
    o_ref[0] = x_ref[0] + gate_ref[0] * _rms(acc, g_ref[...])


def _out_proj(acts, ws, g, gate, x, name):
    b, s, d = x.shape
    tm = min(512, s)
    n_in = len(acts)
    in_specs = [pl.BlockSpec((1, tm, a.shape[2]), lambda bi, i: (bi, i, 0)) for a in acts]
    in_specs += [pl.BlockSpec(w.shape, lambda bi, i: (0, 0)) for w in ws]
    in_specs += [pl.BlockSpec((1, d), lambda bi, i: (0, 0)),
                 pl.BlockSpec((1, 1, d), lambda bi, i: (bi, 0, 0)),
                 pl.BlockSpec((1, tm, d), lambda bi, i: (bi, i, 0))]
    return pl.pallas_call(
        functools.partial(_out_proj_kernel, n_in=n_in),
        grid=(b, s // tm),
        in_specs=in_specs,
        out_specs=pl.BlockSpec((1, tm, d), lambda bi, i: (bi, i, 0)),
        out_shape=jax.ShapeDtypeStruct((b, s, d), F32),
        compiler_params=_cparams("parallel", "parallel"),
        name=name,
    )(*acts, *ws, g.reshape(1, d), gate.reshape(b, 1, d), x)


def _shift_rows(cur, prev, k):
    sc = pltpu.roll(cur, k, 0)
    sp = pltpu.roll(prev, k, 0)
    row8 = lax.broadcasted_iota(jnp.int32, sp.shape, 0)
    head = jnp.where(row8 < k, sp, sc[:8])
    return jnp.concatenate([head, sc[8:]], axis=0)


def _gelu_tanh_x2(x):
    c = math.sqrt(2.0 / math.pi)
    return x * (1.0 + jnp.tanh(x * (c + (c * 0.044715) * (x * x))))


def _ffn_up_kernel(x_ref, xp_ref, g_ref, sc_ref, sh_ref, w_ref, cw_ref, cb_ref, o_ref, u_ref, *, dff, ck):
    def modulated(xv):
        return (_rms(xv, g_ref[...]) * (1.0 + sc_ref[0]) + sh_ref[0]).astype(BF16)

    tm = x_ref.shape[1]
    h = modulated(x_ref[0])
    hp = modulated(xp_ref[0])
    first = pl.program_id(1) == 0

    def conv(slot, col0):
        w_up = w_ref[:, col0:col0 + ck]
        cur = _dot(h, w_up)
        u_ref[slot, 0:8, :] = jnp.where(first, 0.0, _dot(hp, w_up))
        u_ref[slot, 8:8 + tm, :] = cur
        w = cw_ref[:, col0:col0 + ck]
        y = w[2:3] * cur + cb_ref[:, col0:col0 + ck]
        y = y + w[1:2] * u_ref[slot, 7:7 + tm, :]
        return y + w[0:1] * u_ref[slot, 6:6 + tm, :]

    for c in range(dff // ck):
        act = _gelu_tanh_x2(conv(0, c * ck)) * conv(1, dff + c * ck)
        o_ref[0, :, c * ck:(c + 1) * ck] = act.astype(o_ref.dtype)


def _ffn_up(x, g, scale, shift, w_up, conv_w, conv_b):
    b, s, d = x.shape
    c2 = w_up.shape[1]
    dff = c2 // 2
    tm = min(512, s)
    half = jnp.concatenate([jnp.ones((dff,), F32), jnp.full((dff,), 0.5, F32)])
    cw = jnp.pad(conv_w * half, ((0, 8 - conv_w.shape[0]), (0, 0)))
    conv_b = conv_b * half
    return pl.pallas_call(
        functools.partial(_ffn_up_kernel, dff=dff, ck=256),
        grid=(b, s // tm),
        in_specs=[pl.BlockSpec((1, tm, d), lambda bi, i: (bi, i, 0)),
                  pl.BlockSpec((1, 8, d), lambda bi, i: (bi, jnp.maximum(i * (tm // 8) - 1, 0), 0)),
                  pl.BlockSpec((1, d), lambda bi, i: (0, 0)),
                  pl.BlockSpec((1, 1, d), lambda bi, i: (bi, 0, 0)),
                  pl.BlockSpec((1, 1, d), lambda bi, i: (bi, 0, 0)),
                  pl.BlockSpec((d, c2), lambda bi, i: (0, 0)),
                  pl.BlockSpec((8, c2), lambda bi, i: (0, 0)),
                  pl.BlockSpec((1, c2), lambda bi, i: (0, 0))],
        out_specs=pl.BlockSpec((1, tm, dff), lambda bi, i: (bi, i, 0)),
        out_shape=jax.ShapeDtypeStruct((b, s, dff), BF16),
        scratch_shapes=[pltpu.VMEM((2, 8 + tm, 256), F32)],
        compiler_params=_cparams("parallel", "parallel"),
        name="ffn_up_conv_act",
    )(x, x, g.reshape(1, d), scale.reshape(b, 1, d), shift.reshape(b, 1, d), w_up, cw, conv_b.reshape(1, c2))


SBQ = 512


SB_TILES = 4


def _sb_kernel(q_ref, k_ref, v_ref, u_ref, o_ref, acc_ref, cs_ref):
    i = pl.program_id(2)
    qp = q_ref[0]
    lane_lo = lax.broadcasted_iota(jnp.int32, (SBQ, LANES), 1) < HEAD_DIM
    zero = jnp.zeros_like(qp)
    qs = [jnp.where(lane_lo, qp, zero), jnp.where(lane_lo, zero, qp)]
    nsub = SBQ // QT
    diff = (lax.broadcasted_iota(jnp.int32, (SBQ, QT), 1) - lax.broadcasted_iota(jnp.int32, (SBQ, QT), 0))
    acc_ref[...] = jnp.zeros(acc_ref.shape, F32)
    cs_ref[...] = jnp.zeros(cs_ref.shape, F32)

    def tiles(kidxs, diag):
        offs = [pl.multiple_of(kidx * QT, QT) for kidx in kidxs]
        ks = [k_ref[0, pl.ds(k0, QT), :] for k0 in offs]
        zs = [[_dot_nt(q, k) for k in ks] for q in qs]
        lss, css = [[], []], [[], []]
        belows = [diff < i * SBQ - kidx * QT for kidx in kidxs] if diag else None
        for h in range(2):
            for t, z in enumerate(zs[h]):
                ls = jnp.minimum(z, 0.0) - jnp.log(1.0 + jnp.exp(-jnp.abs(z)))
                lr = ls - z
                if diag:
                    lr = jnp.where(belows[t], lr, 0.0)
                hi = lr.astype(BF16)
                lo = (lr - hi.astype(F32)).astype(BF16)
                css[h].append(_dot(jnp.concatenate([hi, lo], axis=1), u_ref[...]))
                lss[h].append(ls)
        vs = [v_ref[0, pl.ds(k0, QT), :] for k0 in offs]
        for h in range(2):
            csum = cs_ref[h]
            acc = acc_ref[h]
            for t, (ls, cs) in enumerate(zip(lss[h], css[h])):
                w = jnp.exp(ls + cs[:, :QT] + csum)
                if diag:
                    w = jnp.where(belows[t], w, 0.0)
                acc = acc + _dot(w.astype(BF16), vs[t])
                csum = csum + cs[:, QT:]
            acc_ref[h] = acc
            cs_ref[h] = csum

    for r0 in range(nsub - 1, -1, -SB_TILES):
        tiles([i * nsub + r0 - r for r in range(SB_TILES)], True)

    def body(it, carry):
        tiles([i * nsub - 1 - it * SB_TILES - r for r in range(SB_TILES)], False)
        return carry

    lax.fori_loop(0, i * (nsub // SB_TILES), body, 0)
    o_ref[0] = jnp.where(lane_lo, acc_ref[0], acc_ref[1]).astype(o_ref.dtype)


def _suffix_matrix():
    j = np.arange(2 * QT)[:, None] % QT
    s = np.arange(QT + LANES)[None, :]
    return jnp.asarray(np.where(s < QT, j > s, True).astype(np.float32), dtype=BF16)


def _sb_attention(proj, q_col, k_col, v_col):
    b, s, _ = proj.shape
    pairs = SB_HEADS // 2
    qb, kb, vb = q_col // LANES, k_col // LANES, v_col // LANES
    return pl.pallas_call(
        _sb_kernel,
        grid=(b, pairs, s // SBQ),
        in_specs=[pl.BlockSpec((1, SBQ, LANES), lambda bi, p, i: (bi, i, qb + p)),
                  pl.BlockSpec((1, s, LANES), lambda bi, p, i: (bi, 0, kb + p)),
                  pl.BlockSpec((1, s, LANES), lambda bi, p, i: (bi, 0, vb + p)),
                  pl.BlockSpec((2 * QT, QT + LANES), lambda bi, p, i: (0, 0))],
        out_specs=pl.BlockSpec((1, SBQ, LANES), lambda bi, p, i: (bi, i, p)),
        out_shape=jax.ShapeDtypeStruct((b, s, SB_WIDTH), BF16),
        scratch_shapes=[pltpu.VMEM((2, SBQ, LANES), F32), pltpu.VMEM((2, SBQ, LANES), F32)],
        compiler_params=_cparams("parallel", "parallel", "arbitrary"),
        name="sb_attention",
    )(proj, proj, proj, _suffix_matrix())


def _pair(f, p, lane_lo):
    return jnp.where(lane_lo, f[:, 2 * p:2 * p + 1], f[:, 2 * p + 1:2 * p + 2])


def _ssd_kernel(xr_ref, xp_ref, dt_ref, z_ref, cw_ref, cb_ref, dtb_ref, alog_ref, dsk_ref, nrm_ref,
                o_ref, st_ref, xc_ref, y_ref):
    ln = SSM_CHUNK
    c = pl.program_id(1)

    @pl.when(c == 0)
    def _():
        st_ref[...] = jnp.zeros(st_ref.shape, F32)

    first = c == 0
    ck = 256
    for j in range(SSM_CONV_DIM // ck):
        cur = xr_ref[0, :, j * ck:(j + 1) * ck].astype(F32)
        prev = xp_ref[0, :, j * ck:(j + 1) * ck].astype(F32)
        prev = jnp.where(first, 0.0, prev)
        w = cw_ref[:, j * ck:(j + 1) * ck]
        y = w[3:4] * cur + cb_ref[:, j * ck:(j + 1) * ck]
        for kk in range(1, SSM_CONV):
            y = y + w[3 - kk:4 - kk] * _shift_rows(cur, prev, kk)
        xc_ref[:, j * ck:(j + 1) * ck] = y * _sigmoid(y)

    dtr = dt_ref[0] + dtb_ref[...]
    dt = jnp.maximum(dtr, 0.0) + jnp.log(1.0 + jnp.exp(-jnp.abs(dtr)))
    a = dt * (-jnp.exp(alog_ref[...]))
    row = lax.broadcasted_iota(jnp.int32, (ln, ln), 0)
    col = lax.broadcasted_iota(jnp.int32, (ln, ln), 1)
    causal = col <= row
    acs = jnp.dot(causal.astype(F32), a, precision=lax.Precision.HIGHEST, preferred_element_type=F32)
    acs_t = acs.T
    eacs = jnp.exp(acs)
    dout = jnp.exp(acs[ln - 1:ln, :] - acs)
    lane_lo = lax.broadcasted_iota(jnp.int32, (ln, LANES), 1) < HEAD_DIM

    hg = SSM_HEADS // SSM_GROUPS
    for g in range(SSM_GROUPS):
        bm = xc_ref[:, SSM_INNER + g * SSM_STATE:SSM_INNER + (g + 1) * SSM_STATE]
        cm = xc_ref[:, SSM_INNER + (SSM_GROUPS + g) * SSM_STATE:SSM_INNER + (SSM_GROUPS + g + 1) * SSM_STATE]
        cmb = cm.astype(BF16)
        cb = _dot_nt(cmb, bm.astype(BF16))
        prev = st_ref[g]
        yoff = _dot(cmb, prev.astype(BF16))
        dox_parts = []
        cdec_parts = []
        for pp in range(hg // 2):
            p = g * (hg // 2) + pp
            xs = xc_ref[:, p * LANES:(p + 1) * LANES]
            xdt = xs * _pair(dt, p, lane_lo)
            ms = []
            for hh in (2 * p, 2 * p + 1):
                seg = acs[:, hh:hh + 1] - acs_t[hh:hh + 1, :]
                decay = jnp.exp(jnp.where(causal, seg, NEG))
                ms.append((cb * decay).astype(BF16))
            rhs = jnp.concatenate([jnp.where(lane_lo, xdt, 0.0), jnp.where(lane_lo, 0.0, xdt)], axis=0)
            ydiag = _dot(jnp.concatenate(ms, axis=1), rhs.astype(BF16))
            yo = yoff[:, pp * LANES:(pp + 1) * LANES] * _pair(eacs, p, lane_lo)
            y_ref[:, p * LANES:(p + 1) * LANES] = ydiag + yo + xs * dsk_ref[:, p * LANES:(p + 1) * LANES]
            dox_parts.append((xdt * _pair(dout, p, lane_lo)).astype(BF16))
            cdec_parts.append(_pair(eacs[ln - 1:ln, :], p, lane_lo[0:1]))
        states = _dot(bm.T.astype(BF16), jnp.concatenate(dox_parts, axis=1))
        st_ref[g] = prev * jnp.concatenate(cdec_parts, axis=1) + states

    zf = z_ref[0].astype(F32)
    yz = y_ref[...] * (zf * _sigmoid(zf))
    o_ref[0] = _rms(yz, nrm_ref[...]).astype(o_ref.dtype)


def _ssd(proj, dt_raw, conv_w, conv_b, dt_bias, a_log, d_skip, ssm_norm):
    b, s, _ = proj.shape
    ln = SSM_CHUNK
    cw = jnp.pad(conv_w, ((0, 8 - SSM_CONV), (0, 0)))
    pad = LANES - SSM_HEADS
    return pl.pallas_call(
        _ssd_kernel,
        grid=(b, s // ln),
        in_specs=[pl.BlockSpec((1, ln, SSM_CONV_DIM), lambda bi, c: (bi, c, 0)),
                  pl.BlockSpec((1, 8, SSM_CONV_DIM), lambda bi, c: (bi, jnp.maximum(c * (ln // 8) - 1, 0), 0)),
                  pl.BlockSpec((1, ln, LANES), lambda bi, c: (bi, c, 0)),
                  pl.BlockSpec((1, ln, SSM_INNER), lambda bi, c: (bi, c, SSM_CONV_DIM // SSM_INNER)),
                  pl.BlockSpec((8, SSM_CONV_DIM), lambda bi, c: (0, 0)),
                  pl.BlockSpec((1, SSM_CONV_DIM), lambda bi, c: (0, 0)),
                  pl.BlockSpec((1, LANES), lambda bi, c: (0, 0)),
                  pl.BlockSpec((1, LANES), lambda bi, c: (0, 0)),
                  pl.BlockSpec((1, SSM_INNER), lambda bi, c: (0, 0)),
                  pl.BlockSpec((1, SSM_INNER), lambda bi, c: (0, 0))],
        out_specs=pl.BlockSpec((1, ln, SSM_INNER), lambda bi, c: (bi, c, 0)),
        out_shape=jax.ShapeDtypeStruct((b, s, SSM_INNER), BF16),
        scratch_shapes=[pltpu.VMEM((SSM_GROUPS, SSM_STATE, SSM_INNER // SSM_GROUPS), F32),
                        pltpu.VMEM((ln, SSM_CONV_DIM), F32),
                        pltpu.VMEM((ln, SSM_INNER), F32)],
        compiler_params=_cparams("parallel", "arbitrary"),
        name="ssd",
    )(proj, proj, dt_raw, proj, cw, conv_b.reshape(1, -1),
      jnp.pad(dt_bias, (0, pad)).reshape(1, LANES), jnp.pad(a_log, (0, pad)).reshape(1, LANES),
      jnp.repeat(d_skip, HEAD_DIM).reshape(1, SSM_INNER), ssm_norm.reshape(1, SSM_INNER))


def _rel_bucket(n):
    n = jnp.maximum(n, 0)
    nf = jnp.maximum(n, 1).astype(F32)
    large = REL_MAX_EXACT + (jnp.log(nf / REL_MAX_EXACT) / math.log(REL_MAX_DIST / REL_MAX_EXACT)
                             * (REL_BUCKETS - REL_MAX_EXACT)).astype(jnp.int32)
    large = jnp.minimum(large, REL_BUCKETS - 1)
    return jnp.where(n < REL_MAX_EXACT, n, large)


def _bias_lookup(rb_ref, head, bucket):
    out = jnp.full(bucket.shape, rb_ref[head, 0], F32)
    for kk in range(1, REL_BUCKETS):
        out = jnp.where(bucket >= kk, rb_ref[head, kk], out)
    return out


N_NEAR = 8


def _bias_tiles_kernel(rb_ref, o_ref):
    h = pl.program_id(0)
    row = lax.broadcasted_iota(jnp.int32, (QT, QT), 0)
    col = lax.broadcasted_iota(jnp.int32, (QT, QT), 1)
    for d in range(N_NEAR):
        bucket = _rel_bucket(d * QT + col - row)
        o_ref[0, d] = _bias_lookup(rb_ref, h, bucket) - rb_ref[h, REL_BUCKETS - 1]


def _bias_tiles(rel_bias):
    return pl.pallas_call(
        _bias_tiles_kernel,
        grid=(NSA_HEADS,),
        in_specs=[pl.BlockSpec(memory_space=pltpu.SMEM)],
        out_specs=pl.BlockSpec((1, N_NEAR, QT, QT), lambda h: (h, 0, 0, 0)),
        out_shape=jax.ShapeDtypeStruct((NSA_HEADS, N_NEAR, QT, QT), F32),
        compiler_params=_cparams("parallel"),
        name="nsa_bias_tiles",
    )(rel_bias)


def _compress_kernel(a_ref, w1_ref, b1_ref, w2_ref, pe_ref, o_ref):
    a = a_ref[0, 0, 0]
    ncp = a.shape[0]
    half = CMP_STRIDE * HEAD_DIM
    h1 = _dot(a, w1_ref[0, :half, :])
    h2 = _dot(a, w1_ref[0, half:, :])
    h2 = pltpu.roll(h2, ncp - 1, 0)
    rowi = lax.broadcasted_iota(jnp.int32, h2.shape, 0)
    h2 = jnp.where(rowi < ncp - 1, h2, 0.0)
    pev = _dot(pe_ref[0], w1_ref[0])[0:1]
    hid = h1 + h2 + pev + b1_ref[0]
    act = hid * _sigmoid(hid)
    o_ref[0, 0, 0] = _dot(act.astype(BF16), w2_ref[0]).astype(o_ref.dtype)


def _compress(a, w1, b1, w2, pe):
    _, b, g, ncp, kk = a.shape
    hid = w1.shape[2]
    pe8 = jnp.pad(pe.reshape(2, 1, CMP_BLOCK * HEAD_DIM), ((0, 0), (0, 7), (0, 0))).astype(BF16)
    return pl.pallas_call(
        _compress_kernel,
        grid=(2, b, g),
        in_specs=[pl.BlockSpec((1, 1, 1, ncp, kk), lambda t, bi, gi: (t, bi, gi, 0, 0)),
                  pl.BlockSpec((1, 2 * kk, hid), lambda t, bi, gi: (t, 0, 0)),
                  pl.BlockSpec((1, 1, hid), lambda t, bi, gi: (t, 0, 0)),
                  pl.BlockSpec((1, hid, HEAD_DIM), lambda t, bi, gi: (t, 0, 0)),
                  pl.BlockSpec((1, 8, 2 * kk), lambda t, bi, gi: (t, 0, 0))],
        out_specs=pl.BlockSpec((1, 1, 1, ncp, HEAD_DIM), lambda t, bi, gi: (t, bi, gi, 0, 0)),
        out_shape=jax.ShapeDtypeStruct((2, b, g, ncp, HEAD_DIM), BF16),
        compiler_params=_cparams("parallel", "parallel", "parallel"),
        name="nsa_compress",
    )(a, w1, b1.reshape(2, 1, hid), w2, pe8)


BUCKET_CONST_FROM = 800


def _cattn_kernel(rb_ref, qt_ref, k_ref, vt_ref, ov_ref, oc_ref, sel_ref, bias_ref):
    g = pl.program_id(0)
    i = pl.program_id(1)
    nb = k_ref.shape[0]
    ncp = k_ref.shape[2]
    q0 = i * QT
    for r in range(ncp // QT):
        rows = slice(r * QT, (r + 1) * QT)
        min_rel = q0 - (CMP_STRIDE * (r * QT + QT - 1) + CMP_BLOCK - 1)
        max_rel = q0 + QT - 1 - (CMP_STRIDE * r * QT + CMP_BLOCK - 1)

        @pl.when((min_rel < BUCKET_CONST_FROM) & (max_rel >= 0))
        def _(r=r, rows=rows):
            t_r = q0 + lax.broadcasted_iota(jnp.int32, (QT, QT), 1)
            cend_r = (r * QT + lax.broadcasted_iota(jnp.int32, (QT, QT), 0)) * CMP_STRIDE + (CMP_BLOCK - 1)
            bucket = _rel_bucket(t_r - cend_r)
            for h in range(NSA_HG):
                bias_ref[h, rows, :] = jnp.where(cend_r <= t_r, _bias_lookup(rb_ref, g * NSA_HG + h, bucket), NEG)

        @pl.when(min_rel >= BUCKET_CONST_FROM)
        def _(rows=rows):
            for h in range(NSA_HG):
                bias_ref[h, rows, :] = jnp.full((QT, QT), rb_ref[g * NSA_HG + h, REL_BUCKETS - 1], F32)

        @pl.when(max_rel < 0)
        def _(rows=rows):
            for h in range(NSA_HG):
                bias_ref[h, rows, :] = jnp.full((QT, QT), NEG, F32)

    anyvalid = q0 + lax.broadcasted_iota(jnp.int32, (1, QT), 1) >= CMP_BLOCK - 1
    qks = [_dot(k_ref[bi, 0], qt_ref[bi, 0, 0]) for bi in range(nb)]
    pcsums = []
    for bi in range(nb):
        vt = vt_ref[bi, 0]
        pcsum = jnp.zeros((ncp, QT), F32)
        ocs = []
        for h in range(NSA_HG):
            lg = qks[bi][:, h * QT:(h + 1) * QT] + bias_ref[h]
            m = jnp.max(lg, axis=0, keepdims=True)
            e = jnp.exp(lg - m)
            p = e * jnp.where(anyvalid, 1.0 / jnp.sum(e, axis=0, keepdims=True), 0.0)
            pcsum = pcsum + p
            ocs.append(_dot(vt, p.astype(BF16)))
        oc_ref[bi, 0, 0] = jnp.concatenate(ocs, axis=1).astype(oc_ref.dtype)
        pcsums.append(pcsum)

    imps = []
    for pcsum in pcsums:
        hi = pcsum.astype(BF16)
        lo = (pcsum - hi.astype(F32)).astype(BF16)
        imps.append(_dot(ov_ref[...], hi) + _dot(ov_ref[...], lo))
    jrow = lax.broadcasted_iota(jnp.int32, (LANES, QT), 0)
    tq = q0 + lax.broadcasted_iota(jnp.int32, (LANES, QT), 1)
    cur = jnp.right_shift(tq, SEL_BLOCK.bit_length() - 1)
    forced = (jrow == 0) | (jrow == cur) | (jrow == cur - 1)
    jf = jrow.astype(F32)
    for bi, imp in enumerate(imps):
        val = jnp.where(forced, FORCE, jnp.where(jrow * SEL_BLOCK <= tq, imp, -1.0))
        sel = jnp.zeros((LANES, QT), F32)
        for _ in range(N_SEL):
            m = jnp.max(val, axis=0, keepdims=True)
            jm = jnp.min(jnp.where(val == m, jf, 1e9), axis=0, keepdims=True)
            hit = jf == jm
            sel = jnp.where(hit, 1.0, sel)
            val = jnp.where(hit, M_INIT, val)
        sel_ref[bi, 0, 0] = ((sel - 1.0) * SEL_MASK).astype(sel_ref.dtype)


def _overlap_t(ncp):
    ci = (np.arange(ncp) * CMP_STRIDE)[None, :]
    sj = (np.arange(LANES) * SEL_BLOCK)[:, None]
    return jnp.asarray(((ci < sj + SEL_BLOCK) & (ci + CMP_BLOCK > sj)).astype(np.float32), dtype=BF16)


def _cattn(rel_bias, qt, kcmp, vcmp_t):
    b, g, nq, d, cols = qt.shape
    ncp = kcmp.shape[2]
    hg = NSA_HG
    q_spec = pl.BlockSpec((b, 1, 1, d, cols), lambda gi, i: (0, gi, i, 0, 0))
    return pl.pallas_call(
        _cattn_kernel,
        grid=(g, nq),
        in_specs=[pl.BlockSpec(memory_space=pltpu.SMEM),
                  q_spec,
                  pl.BlockSpec((b, 1, ncp, d), lambda gi, i: (0, gi, 0, 0)),
                  pl.BlockSpec((b, 1, d, ncp), lambda gi, i: (0, gi, 0, 0)),
                  pl.BlockSpec((LANES, ncp), lambda gi, i: (0, 0))],
        out_specs=[q_spec,
                   pl.BlockSpec((b, 1, 1, LANES, QT), lambda gi, i: (0, gi, i, 0, 0))],
        out_shape=[jax.ShapeDtypeStruct((b, g, nq, d, cols), BF16),
                   jax.ShapeDtypeStruct((b, g, nq, LANES, QT), BF16)],
        scratch_shapes=[pltpu.VMEM((hg, ncp, QT), F32)],
        compiler_params=_cparams("parallel", "parallel"),
        name="nsa_cmp_attn",
    )(rel_bias, qt, kcmp, vcmp_t, _overlap_t(ncp))


VROWS = 80


def _sattn_kernel(qt_ref, sel_ref, ks_ref, vs_ref, kw_ref, vw_ref, tt_ref, oc_ref, gt_ref, o_ref,
                  m_ref, acc_ref, s0_ref, s1_ref):
    i = pl.program_id(2)
    cols = NSA_HG * QT
    qt = qt_ref[0, 0, 0]
    qa = jnp.concatenate([qt, jnp.concatenate([sel_ref[0, 0, 0]] * NSA_HG, axis=1)], axis=0)
    krow = lax.broadcasted_iota(jnp.int32, (QT, cols), 0)
    qcol = lax.broadcasted_iota(jnp.int32, (QT, cols), 1) & (QT - 1)

    def reset():
        m_ref[...] = jnp.full(m_ref.shape, M_INIT, F32)
        acc_ref[...] = jnp.zeros(acc_ref.shape, F32)

    def update(*tiles):
        m = m_ref[...]
        acc = acc_ref[...]
        for s, vt in tiles:
            m_new = jnp.maximum(m, jnp.max(s, axis=0, keepdims=True))
            p = jnp.exp(s - m_new[0:1])
            acc = acc * jnp.exp(m - m_new)[0:1] + _dot(vt, p.astype(BF16))
            m = m_new
        acc_ref[...] = acc
        m_ref[...] = m

    def near_bias(d):
        return jnp.concatenate([tt_ref[h, d] for h in range(NSA_HG)], axis=1)

    def finish():
        acc = acc_ref[...]
        return acc[:HEAD_DIM] / acc[HEAD_DIM:HEAD_DIM + 1]

    def sel_tile(kt, d, size=QT):
        k0 = pl.multiple_of(kt * QT, QT)
        s = _dot(ks_ref[0, 0, pl.ds(k0, size), :], qa)
        if d is not None:
            s = s + near_bias(d)
        return s, vs_ref[0, 0, :, pl.ds(k0, size)]

    def group(k_ref, v_ref, q_op, kt0, ds, first_gt, last_le):
        n = len(ds)
        k0 = pl.multiple_of(kt0 * QT, QT)
        s = _dot(k_ref[0, 0, pl.ds(k0, n * QT), :], q_op)
        parts = []
        for t, d in enumerate(ds):
            blk = s[t * QT:(t + 1) * QT] + near_bias(d)
            if t == 0 and first_gt:
                blk = jnp.where(krow > qcol, blk, NEG)
            if t == n - 1 and last_le:
                blk = jnp.where(krow <= qcol, blk, NEG)
            parts.append(blk)
        return jnp.concatenate(parts, axis=0), v_ref[0, 0, :, pl.ds(k0, n * QT)]

    reset()
    big = (N_NEAR // 2) * QT

    @pl.when(i >= N_NEAR - 1)
    def _():
        n_far = i - (N_NEAR - 1)
        n_chunks = (n_far + N_NEAR - 1) // N_NEAR

        def start(c):
            return jnp.maximum(n_far - N_NEAR * (n_chunks - c), 0)

        def logits(c, dst_ref):
            for hf in range(2):
                k0 = pl.multiple_of(start(c) * QT + hf * big, QT)
                dst_ref[hf] = _dot(ks_ref[0, 0, pl.ds(k0, big), :], qa)

        def values(c, hf):
            k0 = pl.multiple_of(start(c) * QT + hf * big, QT)
            return vs_ref[0, 0, :, pl.ds(k0, big)]

        key_lane = lax.broadcasted_iota(jnp.int32, (VROWS, big), 1)

        def far_step(c, src_ref, dst_ref):
            logits(c + 1, dst_ref)
            own = (start(c + 1) - start(c)) * QT
            tiles = []
            for hf in range(2):
                vt = values(c, hf)
                tiles.append((src_ref[hf], jnp.where(key_lane < own - hf * big, vt, jnp.zeros_like(vt))))
            update(*tiles)

        def near_step(src_ref):
            tiles = []
            for hf in range(2):
                parts = []
                for t in range(N_NEAR // 2):
                    d = N_NEAR - 1 - (hf * (N_NEAR // 2) + t)
                    blk = src_ref[hf, t * QT:(t + 1) * QT, :] + near_bias(d)
                    if d == 0:
                        blk = jnp.where(krow <= qcol, blk, NEG)
                    parts.append(blk)
                tiles.append((jnp.concatenate(parts, axis=0), values(n_chunks, hf)))
            update(*tiles)

        logits(0, s0_ref)

        def far_body(j, carry):
            far_step(2 * j, s0_ref, s1_ref)
            far_step(2 * j + 1, s1_ref, s0_ref)
            return carry

        lax.fori_loop(0, n_chunks // 2, far_body, 0)

        @pl.when(n_chunks % 2 == 1)
        def _():
            far_step(n_chunks - 1, s0_ref, s1_ref)
            near_step(s1_ref)

        @pl.when(n_chunks % 2 == 0)
        def _():
            near_step(s0_ref)

    def near_body(kt, carry):
        update(sel_tile(kt, i - kt))
        return carry

    @pl.when(i < N_NEAR - 1)
    def _():
        lax.fori_loop(0, i, near_body, 0)
        s, vt = sel_tile(i, 0)
        update((jnp.where(krow <= qcol, s, NEG), vt))

    osl = finish()

    reset()
    nwin = WINDOW // QT

    @pl.when(i >= nwin)
    def _():
        update(group(kw_ref, vw_ref, qt, i - nwin, list(range(nwin, -1, -1)), True, True))

    @pl.when(i < nwin)
    def _():
        for d in range(nwin - 1, -1, -1):
            @pl.when(i - d >= 0)
            def _(d=d):
                update(group(kw_ref, vw_ref, qt, i - d, [d], False, d == 0))

    ow = finish()

    gt = _sigmoid(gt_ref[0, 0])
    gate = lambda br: jnp.concatenate([gt[3 * h + br:3 * h + br + 1] for h in range(NSA_HG)], axis=1)
    o = gate(0) * oc_ref[0, 0, 0].astype(F32) + gate(1) * osl + gate(2) * ow
    o_ref[0, 0, 0] = o.astype(o_ref.dtype)


def _sattn(qt, selb, ksa, vst, kw, vwt, tt, oct, gates_t):
    b, g, nq, d, cols = qt.shape
    s = kw.shape[2]
    hg = NSA_HG
    k_spec = lambda width: pl.BlockSpec((1, 1, s, width), lambda bi, gi, i: (bi, gi, 0, 0))
    v_spec = pl.BlockSpec((1, 1, VROWS, s), lambda bi, gi, i: (bi, gi, 0, 0))
    q_spec = pl.BlockSpec((1, 1, 1, d, cols), lambda bi, gi, i: (bi, gi, i, 0, 0))
    return pl.pallas_call(
        _sattn_kernel,
        grid=(b, g, nq),
        in_specs=[q_spec,
                  pl.BlockSpec((1, 1, 1, LANES, QT), lambda bi, gi, i: (bi, gi, i, 0, 0)),
                  k_spec(ksa.shape[3]), v_spec, k_spec(d), v_spec,
                  pl.BlockSpec((hg, N_NEAR, QT, QT), lambda bi, gi, i: (gi, 0, 0, 0)),
                  q_spec,
                  pl.BlockSpec((1, 1, 16, QT), lambda bi, gi, i: (bi, gi, 0, i))],
        out_specs=q_spec,
        out_shape=jax.ShapeDtypeStruct((b, g, nq, d, cols), BF16),
        scratch_shapes=[pltpu.VMEM((8, cols), F32),
                        pltpu.VMEM((VROWS, cols), F32),
                        pltpu.VMEM((2, (N_NEAR // 2) * QT, cols), F32),
                        pltpu.VMEM((2, (N_NEAR // 2) * QT, cols), F32)],
        compiler_params=_cparams("parallel", "parallel", "arbitrary"),
        name="nsa_sel_win_attn",
    )(qt, selb, ksa, vst, kw, vwt, tt, oct, gates_t)


def _heads(t, h):
    b, s, _ = t.shape
    return t.reshape(b, s, h, HEAD_DIM).transpose(0, 2, 1, 3)


def _mixer_ab(x, g, scale, shift, w_in, conv_w, conv_b, dt_bias, a_log, d_skip, ssm_norm):
    b, s, d = x.shape
    o3 = 3 * SB_WIDTH
    qs = HEAD_DIM ** -0.5
    w_q, w_k, w_v = w_in[:, :SB_WIDTH] * qs, w_in[:, SB_WIDTH:2 * SB_WIDTH], w_in[:, 2 * SB_WIDTH:o3]
    w_z = w_in[:, o3:o3 + SSM_INNER]
    w_xbc = w_in[:, o3 + SSM_INNER:o3 + SSM_INNER + SSM_CONV_DIM]
    w_dt = w_in[:, o3 + SSM_INNER + SSM_CONV_DIM:]
    w_main = jnp.concatenate([w_xbc, w_z, w_q, w_k, w_v], axis=1).astype(BF16)
    w_dt = jnp.pad(w_dt, ((0, 0), (0, LANES - SSM_HEADS))).astype(BF16)
    proj = _norm_proj(x, g, scale, shift, w_main, BF16, 512, "ab_in_proj")
    dt_raw = _norm_proj(x, g, scale, shift, w_dt, F32, LANES, "ab_dt_proj")
    c0 = SSM_CONV_DIM + SSM_INNER
    o_sb = _sb_attention(proj, c0, c0 + SB_WIDTH, c0 + 2 * SB_WIDTH)
    y = _ssd(proj, dt_raw, conv_w, conv_b, dt_bias, a_log, d_skip, ssm_norm)
    return o_sb, y


def _mixer_nsa(x, g, scale, shift, tt, rel_bias, w_in, cmp_w1, cmp_b1, cmp_w2, cmp_pe):
    b, s, d = x.shape
    gq, hg = NSA_GROUPS, NSA_HG
    qs = HEAD_DIM ** -0.5
    n_main = NSA_Q + 6 * NSA_KV
    w_main = jnp.concatenate([w_in[:, :NSA_Q] * qs, w_in[:, NSA_Q:n_main]], axis=1).astype(BF16)
    n_gate = NSA_HEADS * N_BRANCH
    w_gate = jnp.pad(w_in[:, n_main:], ((0, 0), (0, LANES - n_gate))).astype(BF16)
    proj = _norm_proj(x, g, scale, shift, w_main, BF16, 512, "nsa_in_proj")
    gates = _norm_proj(x, g, scale, shift, w_gate, F32, LANES, "nsa_gate_proj")[..., :n_gate]
    gates_t = gates.reshape(b, s, gq, hg * N_BRANCH).transpose(0, 2, 3, 1)
    gates_t = jnp.pad(gates_t, ((0, 0), (0, 0), (0, 16 - hg * N_BRANCH), (0, 0)))
    nq = s // QT
    q5 = proj[..., :NSA_Q].reshape(b, nq, QT, gq, hg, HEAD_DIM)
    qt = q5.transpose(0, 3, 1, 5, 4, 2).reshape(b, gq, nq, HEAD_DIM, hg * QT)
    part = lambda j: proj[..., NSA_Q + j * NSA_KV:NSA_Q + (j + 1) * NSA_KV]
    nc = s // CMP_STRIDE
    kvc = jnp.stack([part(0), part(1)])
    a = kvc.reshape(2, b, nc, CMP_STRIDE, gq, HEAD_DIM).transpose(0, 1, 4, 2, 3, 5)
    a = a.reshape(2, b, gq, nc, CMP_STRIDE * HEAD_DIM)
    cmp = _compress(a, cmp_w1.astype(BF16), cmp_b1, cmp_w2.astype(BF16), cmp_pe)
    oct, selb = _cattn(rel_bias, qt, cmp[0], cmp[1].transpose(0, 1, 3, 2))
    grp = lambda t: t.reshape(b, s, gq, HEAD_DIM).transpose(0, 2, 1, 3)
    onehot = (jnp.arange(s)[:, None] // SEL_BLOCK == jnp.arange(LANES)[None, :]).astype(BF16)
    ksa = jnp.concatenate([grp(part(2)), jnp.broadcast_to(onehot, (b, gq, s, LANES))], axis=-1)

    def values_t(t):
        vt = t.reshape(b, s, gq, HEAD_DIM).transpose(0, 2, 3, 1)
        ones = jnp.ones((b, gq, 1, s), BF16)
        return jnp.concatenate([vt, ones, jnp.zeros((b, gq, VROWS - HEAD_DIM - 1, s), BF16)], axis=2)

    ot = _sattn(qt, selb, ksa, values_t(part(3)), grp(part(4)), values_t(part(5)), tt, oct, gates_t)
    o = ot.reshape(b, gq, nq, HEAD_DIM, hg, QT).transpose(0, 2, 5, 1, 4, 3)
    return o.reshape(b, s, NSA_Q)


def kernel(x, c, rel_bias, ada_w, ada_b, norm_g, ab_w_in, ab_conv_w, ab_conv_b, ab_dt_bias, ab_a_log,
           ab_d_skip, ab_ssm_norm, ab_w_out, nsa_w_in, nsa_cmp_w1, nsa_cmp_b1, nsa_cmp_w2, nsa_cmp_pe,
           nsa_w_out, ffn_w_up, ffn_conv_w, ffn_conv_b, ffn_w_down):
    depth = ada_w.shape[0]
    d = x.shape[-1]
    assert x.shape[1] % (CMP_STRIDE * QT) == 0 and x.shape[1] // SEL_BLOCK <= LANES
    mod = _modulation(c, ada_w, ada_b)
    tt = _bias_tiles(rel_bias) if depth > 1 else None
    for l in range(depth):
        shift1, scale1, gate1, shift2, scale2, gate2 = [mod[l, :, j * d:(j + 1) * d] for j in range(6)]
        if l % 2 == 0:
            e = l // 2
            o_sb, y = _mixer_ab(x, norm_g[l, 0], scale1, shift1, ab_w_in[e], ab_conv_w[e], ab_conv_b[e],
                                ab_dt_bias[e], ab_a_log[e], ab_d_skip[e], ab_ssm_norm[e])
            w_out = ab_w_out[e].astype(BF16)
            x = _out_proj([o_sb, y], [w_out[:SB_WIDTH], w_out[SB_WIDTH:]], norm_g[l, 1], gate1, x, "ab_out_proj")
        else:
            o = l // 2
            a = _mixer_nsa(x, norm_g[l, 0], scale1, shift1, tt, rel_bias, nsa_w_in[o], nsa_cmp_w1[o],
                           nsa_cmp_b1[o], nsa_cmp_w2[o], nsa_cmp_pe[o])
            x = _out_proj([a], [nsa_w_out[o].astype(BF16)], norm_g[l, 1], gate1, x, "nsa_out_proj")
        a = _ffn_up(x, norm_g[l, 2], scale2, shift2, ffn_w_up[l].astype(BF16), ffn_conv_w[l], ffn_conv_b[l])
        x = _out_proj([a], [ffn_w_down[l].astype(BF16)], norm_g[l, 3], gate2, x, "ffn_down_proj")
    return x
```

```python
import functools
import math

import numpy as np
import jax
import jax.numpy as jnp
from jax import lax
from jax.experimental import pallas as pl
from jax.experimental.pallas import tpu as pltpu

BF16 = jnp.bfloat16
F32 = jnp.float32

SB_HEADS = 8
HEAD_DIM = 64
SB_WIDTH = SB_HEADS * HEAD_DIM
SSM_HEADS = 16
SSM_INNER = SSM_HEADS * HEAD_DIM
SSM_GROUPS = 4
SSM_STATE = 128
SSM_CONV = 4
SSM_CHUNK = 128
SSM_CONV_DIM = SSM_INNER + 2 * SSM_GROUPS * SSM_STATE
NSA_HEADS = 16
NSA_GROUPS = 4
NSA_HG = NSA_HEADS // NSA_GROUPS
NSA_Q = NSA_HEADS * HEAD_DIM
NSA_KV = NSA_GROUPS * HEAD_DIM
CMP_BLOCK = 32
CMP_STRIDE = 16
SEL_BLOCK = 64
N_SEL = 16
WINDOW = 512
N_BRANCH = 3
REL_BUCKETS = 32
REL_MAX_EXACT = 16
REL_MAX_DIST = 1024
FFN_CONV = 3
EPS = 1e-6
NEG = -1e30
FORCE = 1e4
SEL_MASK = 2.0 ** 100
M_INIT = -3.0e38

QT = 128
LANES = 128
VMEM_LIMIT = 48 * 1024 * 1024


def _cparams(*sem):
    return pltpu.CompilerParams(dimension_semantics=sem, vmem_limit_bytes=VMEM_LIMIT)


def _sigmoid(x):
    return 1.0 / (1.0 + jnp.exp(-x))


def _dot(a, b):
    return jnp.dot(a, b, preferred_element_type=F32)


def _dot_nt(a, b):
    return lax.dot_general(a, b, (((1,), (1,)), ((), ())), preferred_element_type=F32)


def _rms(y, g):
    ms = jnp.mean(y * y, axis=-1, keepdims=True)
    return y * lax.rsqrt(ms + EPS) * g


def _mod_kernel(c_ref, w_ref, b_ref, o_ref):
    c = c_ref[...]
    s = c * _sigmoid(c)
    o_ref[0] = _dot(s.astype(BF16), w_ref[0].astype(BF16)) + b_ref[0]


def _modulation(c, ada_w, ada_b):
    depth, d, n = ada_w.shape
    b = c.shape[0]
    bp = 8
    tn = 768
    cp = jnp.pad(c, ((0, bp - b), (0, 0)))
    out = pl.pallas_call(
        _mod_kernel,
        grid=(depth, n // tn),
        in_specs=[pl.BlockSpec((bp, d), lambda l, j: (0, 0)),
                  pl.BlockSpec((1, d, tn), lambda l, j: (l, 0, j)),
                  pl.BlockSpec((1, 1, tn), lambda l, j: (l, 0, j))],
        out_specs=pl.BlockSpec((1, bp, tn), lambda l, j: (l, 0, j)),
        out_shape=jax.ShapeDtypeStruct((depth, bp, n), F32),
        compiler_params=_cparams("parallel", "parallel"),
        name="adaln_mod",
    )(cp, ada_w, ada_b.reshape(depth, 1, n))
    return out[:, :b]


def _norm_proj_kernel(x_ref, g_ref, sc_ref, sh_ref, w_ref, w2_ref, o_ref, o2_ref, *, tn):
    y = _rms(x_ref[0], g_ref[...])
    h = (y * (1.0 + sc_ref[0]) + sh_ref[0]).astype(BF16)
    for j in range(w_ref.shape[1] // tn):
        o_ref[0, :, j * tn:(j + 1) * tn] = _dot(h, w_ref[:, j * tn:(j + 1) * tn]).astype(o_ref.dtype)
    o2_ref[0] = _dot(h, w2_ref[...])


def _norm_proj(x, g, scale, shift, w, w_f32, tn, name):
    b, s, d = x.shape
    n = w.shape[1]
    tm = min(512, s)
    return pl.pallas_call(
        functools.partial(_norm_proj_kernel, tn=tn),
        grid=(b, s // tm),
        in_specs=[pl.BlockSpec((1, tm, d), lambda bi, i: (bi, i, 0)),
                  pl.BlockSpec((1, d), lambda bi, i: (0, 0)),
                  pl.BlockSpec((1, 1, d), lambda bi, i: (bi, 0, 0)),
                  pl.BlockSpec((1, 1, d), lambda bi, i: (bi, 0, 0)),
                  pl.BlockSpec((d, n), lambda bi, i: (0, 0)),
                  pl.BlockSpec((d, LANES), lambda bi, i: (0, 0))],
        out_specs=[pl.BlockSpec((1, tm, n), lambda bi, i: (bi, i, 0)),
                   pl.BlockSpec((1, tm, LANES), lambda bi, i: (bi, i, 0))],
        out_shape=[jax.ShapeDtypeStruct((b, s, n), BF16), jax.ShapeDtypeStruct((b, s, LANES), F32)],
        compiler_params=_cparams("parallel", "parallel"),
        name=name,
    )(x, g.reshape(1, d), scale.reshape(b, 1, d), shift.reshape(b, 1, d), w, w_f32)


def _out_proj_kernel(*refs, n_in):
    a_refs = refs[:n_in]
    w_refs = refs[n_in:2 * n_in]
    g_ref, gate_ref, x_ref, o_ref = refs[2 * n_in:]
    acc = _dot(a_refs[0][0], w_refs[0][...])
    for a_ref, w_ref in zip(a_refs[1:], w_refs[1:]):
        acc = acc + _dot(a_ref[0], w_ref[...])
    o_ref[0] = x_ref[0] + gate_ref[0] * _rms(acc, g_ref[...])


def _out_proj(acts, ws, g, gate, x, name):
    b, s, d = x.shape
    tm = min(512, s)
    n_in = len(acts)
    in_specs = [pl.BlockSpec((1, tm, a.shape[2]), lambda bi, i: (bi, i, 0)) for a in acts]
    in_specs += [pl.BlockSpec(w.shape, lambda bi, i: (0, 0)) for w in ws]
    in_specs += [pl.BlockSpec((1, d), lambda bi, i: (0, 0)),
                 pl.BlockSpec((1, 1, d), lambda bi, i: (bi, 0, 0)),
                 pl.BlockSpec((1, tm, d), lambda bi, i: (bi, i, 0))]
    return pl.pallas_call(
        functools.partial(_out_proj_kernel, n_in=n_in),
        grid=(b, s // tm),
        in_specs=in_specs,
        out_specs=pl.BlockSpec((1, tm, d), lambda bi, i: (bi, i, 0)),
        out_shape=jax.ShapeDtypeStruct((b, s, d), F32),
        compiler_params=_cparams("parallel", "parallel"),
        name=name,
    )(*acts, *ws, g.reshape(1, d), gate.reshape(b, 1, d), x)


def _shift_rows(cur, prev, k):
    sc = pltpu.roll(cur, k, 0)
    sp = pltpu.roll(prev, k, 0)
    row8 = lax.broadcasted_iota(jnp.int32, sp.shape, 0)
    head = jnp.where(row8 < k, sp, sc[:8])
    return jnp.concatenate([head, sc[8:]], axis=0)


def _gelu_tanh_x2(x):
    c = math.sqrt(2.0 / math.pi)
    return x * (1.0 + jnp.tanh(x * (c + (c * 0.044715) * (x * x))))


def _ffn_up_kernel(x_ref, xp_ref, g_ref, sc_ref, sh_ref, w_ref, cw_ref, cb_ref, o_ref, u_ref, *, dff, ck):
    def modulated(xv):
        return (_rms(xv, g_ref[...]) * (1.0 + sc_ref[0]) + sh_ref[0]).astype(BF16)

    tm = x_ref.shape[1]
    h = modulated(x_ref[0])
    hp = modulated(xp_ref[0])
    first = pl.program_id(1) == 0

    def conv(slot, col0):
        w_up = w_ref[:, col0:col0 + ck]
        cur = _dot(h, w_up)
        u_ref[slot, 0:8, :] = jnp.where(first, 0.0, _dot(hp, w_up))
        u_ref[slot, 8:8 + tm, :] = cur
        w = cw_ref[:, col0:col0 + ck]
        y = w[2:3] * cur + cb_ref[:, col0:col0 + ck]
        y = y + w[1:2] * u_ref[slot, 7:7 + tm, :]
        return y + w[0:1] * u_ref[slot, 6:6 + tm, :]

    for c in range(dff // ck):
        act = _gelu_tanh_x2(conv(0, c * ck)) * conv(1, dff + c * ck)
        o_ref[0, :, c * ck:(c + 1) * ck] = act.astype(o_ref.dtype)


def _ffn_up(x, g, scale, shift, w_up, conv_w, conv_b):
    b, s, d = x.shape
    c2 = w_up.shape[1]
    dff = c2 // 2
    tm = min(512, s)
    half = jnp.concatenate([jnp.ones((dff,), F32), jnp.full((dff,), 0.5, F32)])
    cw = jnp.pad(conv_w * half, ((0, 8 - conv_w.shape[0]), (0, 0)))
    conv_b = conv_b * half
    return pl.pallas_call(
        functools.partial(_ffn_up_kernel, dff=dff, ck=256),
        grid=(b, s // tm),
        in_specs=[pl.BlockSpec((1, tm, d), lambda bi, i: (bi, i, 0)),
                  pl.BlockSpec((1, 8, d), lambda bi, i: (bi, jnp.maximum(i * (tm // 8) - 1, 0), 0)),
                  pl.BlockSpec((1, d), lambda bi, i: (0, 0)),
                  pl.BlockSpec((1, 1, d), lambda bi, i: (bi, 0, 0)),
                  pl.BlockSpec((1, 1, d), lambda bi, i: (bi, 0, 0)),
                  pl.BlockSpec((d, c2), lambda bi, i: (0, 0)),
                  pl.BlockSpec((8, c2), lambda bi, i: (0, 0)),
                  pl.BlockSpec((1, c2), lambda bi, i: (0, 0))],
        out_specs=pl.BlockSpec((1, tm, dff), lambda bi, i: (bi, i, 0)),
        out_shape=jax.ShapeDtypeStruct((b, s, dff), BF16),
        scratch_shapes=[pltpu.VMEM((2, 8 + tm, 256), F32)],
        compiler_params=_cparams("parallel", "parallel"),
        name="ffn_up_conv_act",
    )(x, x, g.reshape(1, d), scale.reshape(b, 1, d), shift.reshape(b, 1, d), w_up, cw, conv_b.reshape(1, c2))


SBQ = 512


SB_TILES = 4


def _sb_kernel(q_ref, k_ref, v_ref, u_ref, o_ref, acc_ref, cs_ref):
    i = pl.program_id(2)
    qp = q_ref[0]
    lane_lo = lax.broadcasted_iota(jnp.int32, (SBQ, LANES), 1) < HEAD_DIM
    zero = jnp.zeros_like(qp)
    qs = [jnp.where(lane_lo, qp, zero), jnp.where(lane_lo, zero, qp)]
    nsub = SBQ // QT
    diff = (lax.broadcasted_iota(jnp.int32, (SBQ, QT), 1) - lax.broadcasted_iota(jnp.int32, (SBQ, QT), 0))
    acc_ref[...] = jnp.zeros(acc_ref.shape, F32)
    cs_ref[...] = jnp.zeros(cs_ref.shape, F32)

    def tiles(kidxs, diag):
        offs = [pl.multiple_of(kidx * QT, QT) for kidx in kidxs]
        ks = [k_ref[0, pl.ds(k0, QT), :] for k0 in offs]
        zs = [[_dot_nt(q, k) for k in ks] for q in qs]
        css = [[], []]
        belows = [diff < i * SBQ - kidx * QT for kidx in kidxs] if diag else None
        for h in range(2):
            for t, z in enumerate(zs[h]):
                nl = jnp.maximum(z, 0.0) + jnp.log(1.0 + jnp.exp(-jnp.abs(z)))
                if diag:
                    nl = jnp.where(belows[t], nl, 0.0)
                hi = nl.astype(BF16)
                lo = (nl - hi.astype(F32)).astype(BF16)
                css[h].append(_dot(jnp.concatenate([hi, lo], axis=1), u_ref[...]))
        vs = [v_ref[0, pl.ds(k0, QT), :] for k0 in offs]
        for h in range(2):
            csum = cs_ref[h]
            acc = acc_ref[h]
            for t, (z, cs) in enumerate(zip(zs[h], css[h])):
                w = jnp.exp(z + cs[:, :QT] + csum)
                if diag:
                    w = jnp.where(belows[t], w, 0.0)
                acc = acc + _dot(w.astype(BF16), vs[t])
                csum = csum + cs[:, QT:]
            acc_ref[h] = acc
            cs_ref[h] = csum

    for r0 in range(nsub - 1, -1, -SB_TILES):
        tiles([i * nsub + r0 - r for r in range(SB_TILES)], True)

    def body(it, carry):
        tiles([i * nsub - 1 - it * SB_TILES - r for r in range(SB_TILES)], False)
        return carry

    lax.fori_loop(0, i * (nsub // SB_TILES), body, 0)
    o_ref[0] = jnp.where(lane_lo, acc_ref[0], acc_ref[1]).astype(o_ref.dtype)


def _suffix_matrix():
    j = np.arange(2 * QT)[:, None] % QT
    s = np.arange(QT + LANES)[None, :]
    return jnp.asarray(-np.where(s < QT, j >= s, True).astype(np.float32), dtype=BF16)


def _sb_attention(proj, q_col, k_col, v_col):
    b, s, _ = proj.shape
    pairs = SB_HEADS // 2
    qb, kb, vb = q_col // LANES, k_col // LANES, v_col // LANES
    return pl.pallas_call(
        _sb_kernel,
        grid=(b, pairs, s // SBQ),
        in_specs=[pl.BlockSpec((1, SBQ, LANES), lambda bi, p, i: (bi, i, qb + p)),
                  pl.BlockSpec((1, s, LANES), lambda bi, p, i: (bi, 0, kb + p)),
                  pl.BlockSpec((1, s, LANES), lambda bi, p, i: (bi, 0, vb + p)),
                  pl.BlockSpec((2 * QT, QT + LANES), lambda bi, p, i: (0, 0))],
        out_specs=pl.BlockSpec((1, SBQ, LANES), lambda bi, p, i: (bi, i, p)),
        out_shape=jax.ShapeDtypeStruct((b, s, SB_WIDTH), BF16),
        scratch_shapes=[pltpu.VMEM((2, SBQ, LANES), F32), pltpu.VMEM((2, SBQ, LANES), F32)],
        compiler_params=_cparams("parallel", "parallel", "arbitrary"),
        name="sb_attention",
    )(proj, proj, proj, _suffix_matrix())


def _pair(f, p, lane_lo):
    return jnp.where(lane_lo, f[:, 2 * p:2 * p + 1], f[:, 2 * p + 1:2 * p + 2])


def _ssd_kernel(xr_ref, xp_ref, dt_ref, z_ref, cw_ref, cb_ref, dtb_ref, alog_ref, dsk_ref, nrm_ref,
                o_ref, st_ref, xc_ref, y_ref):
    ln = SSM_CHUNK
    c = pl.program_id(1)

    @pl.when(c == 0)
    def _():
        st_ref[...] = jnp.zeros(st_ref.shape, F32)

    first = c == 0
    ck = 256
    for j in range(SSM_CONV_DIM // ck):
        cur = xr_ref[0, :, j * ck:(j + 1) * ck].astype(F32)
        prev = xp_ref[0, :, j * ck:(j + 1) * ck].astype(F32)
        prev = jnp.where(first, 0.0, prev)
        w = cw_ref[:, j * ck:(j + 1) * ck]
        y = w[3:4] * cur + cb_ref[:, j * ck:(j + 1) * ck]
        for kk in range(1, SSM_CONV):
            y = y + w[3 - kk:4 - kk] * _shift_rows(cur, prev, kk)
        xc_ref[:, j * ck:(j + 1) * ck] = y * _sigmoid(y)

    dtr = dt_ref[0] + dtb_ref[...]
    dt = jnp.maximum(dtr, 0.0) + jnp.log(1.0 + jnp.exp(-jnp.abs(dtr)))
    a = dt * (-jnp.exp(alog_ref[...]))
    row = lax.broadcasted_iota(jnp.int32, (ln, ln), 0)
    col = lax.broadcasted_iota(jnp.int32, (ln, ln), 1)
    causal = col <= row
    acs = jnp.dot(causal.astype(F32), a, precision=lax.Precision.HIGHEST, preferred_element_type=F32)
    acs_t = acs.T
    eacs = jnp.exp(acs)
    dout = jnp.exp(acs[ln - 1:ln, :] - acs)
    lane_lo = lax.broadcasted_iota(jnp.int32, (ln, LANES), 1) < HEAD_DIM

    hg = SSM_HEADS // SSM_GROUPS
    for g in range(SSM_GROUPS):
        bm = xc_ref[:, SSM_INNER + g * SSM_STATE:SSM_INNER + (g + 1) * SSM_STATE]
        cm = xc_ref[:, SSM_INNER + (SSM_GROUPS + g) * SSM_STATE:SSM_INNER + (SSM_GROUPS + g + 1) * SSM_STATE]
        cmb = cm.astype(BF16)
        cb = _dot_nt(cmb, bm.astype(BF16))
        prev = st_ref[g]
        yoff = _dot(cmb, prev.astype(BF16))
        dox_parts = []
        cdec_parts = []
        for pp in range(hg // 2):
            p = g * (hg // 2) + pp
            xs = xc_ref[:, p * LANES:(p + 1) * LANES]
            xdt = xs * _pair(dt, p, lane_lo)
            ms = []
            for hh in (2 * p, 2 * p + 1):
                seg = acs[:, hh:hh + 1] - acs_t[hh:hh + 1, :]
                decay = jnp.exp(jnp.where(causal, seg, NEG))
                ms.append((cb * decay).astype(BF16))
            rhs = jnp.concatenate([jnp.where(lane_lo, xdt, 0.0), jnp.where(lane_lo, 0.0, xdt)], axis=0)
            ydiag = _dot(jnp.concatenate(ms, axis=1), rhs.astype(BF16))
            yo = yoff[:, pp * LANES:(pp + 1) * LANES] * _pair(eacs, p, lane_lo)
            y_ref[:, p * LANES:(p + 1) * LANES] = ydiag + yo + xs * dsk_ref[:, p * LANES:(p + 1) * LANES]
            dox_parts.append((xdt * _pair(dout, p, lane_lo)).astype(BF16))
            cdec_parts.append(_pair(eacs[ln - 1:ln, :], p, lane_lo[0:1]))
        states = _dot(bm.T.astype(BF16), jnp.concatenate(dox_parts, axis=1))
        st_ref[g] = prev * jnp.concatenate(cdec_parts, axis=1) + states

    zf = z_ref[0].astype(F32)
    yz = y_ref[...] * (zf * _sigmoid(zf))
    o_ref[0] = _rms(yz, nrm_ref[...]).astype(o_ref.dtype)


def _ssd(proj, dt_raw, conv_w, conv_b, dt_bias, a_log, d_skip, ssm_norm):
    b, s, _ = proj.shape
    ln = SSM_CHUNK
    cw = jnp.pad(conv_w, ((0, 8 - SSM_CONV), (0, 0)))
    pad = LANES - SSM_HEADS
    return pl.pallas_call(
        _ssd_kernel,
        grid=(b, s // ln),
        in_specs=[pl.BlockSpec((1, ln, SSM_CONV_DIM), lambda bi, c: (bi, c, 0)),
                  pl.BlockSpec((1, 8, SSM_CONV_DIM), lambda bi, c: (bi, jnp.maximum(c * (ln // 8) - 1, 0), 0)),
                  pl.BlockSpec((1, ln, LANES), lambda bi, c: (bi, c, 0)),
                  pl.BlockSpec((1, ln, SSM_INNER), lambda bi, c: (bi, c, SSM_CONV_DIM // SSM_INNER)),
                  pl.BlockSpec((8, SSM_CONV_DIM), lambda bi, c: (0, 0)),
                  pl.BlockSpec((1, SSM_CONV_DIM), lambda bi, c: (0, 0)),
                  pl.BlockSpec((1, LANES), lambda bi, c: (0, 0)),
                  pl.BlockSpec((1, LANES), lambda bi, c: (0, 0)),
                  pl.BlockSpec((1, SSM_INNER), lambda bi, c: (0, 0)),
                  pl.BlockSpec((1, SSM_INNER), lambda bi, c: (0, 0))],
        out_specs=pl.BlockSpec((1, ln, SSM_INNER), lambda bi, c: (bi, c, 0)),
        out_shape=jax.ShapeDtypeStruct((b, s, SSM_INNER), BF16),
        scratch_shapes=[pltpu.VMEM((SSM_GROUPS, SSM_STATE, SSM_INNER // SSM_GROUPS), F32),
                        pltpu.VMEM((ln, SSM_CONV_DIM), F32),
                        pltpu.VMEM((ln, SSM_INNER), F32)],
        compiler_params=_cparams("parallel", "arbitrary"),
        name="ssd",
    )(proj, proj, dt_raw, proj, cw, conv_b.reshape(1, -1),
      jnp.pad(dt_bias, (0, pad)).reshape(1, LANES), jnp.pad(a_log, (0, pad)).reshape(1, LANES),
      jnp.repeat(d_skip, HEAD_DIM).reshape(1, SSM_INNER), ssm_norm.reshape(1, SSM_INNER))


def _rel_bucket(n):
    n = jnp.maximum(n, 0)
    nf = jnp.maximum(n, 1).astype(F32)
    large = REL_MAX_EXACT + (jnp.log(nf / REL_MAX_EXACT) / math.log(REL_MAX_DIST / REL_MAX_EXACT)
                             * (REL_BUCKETS - REL_MAX_EXACT)).astype(jnp.int32)
    large = jnp.minimum(large, REL_BUCKETS - 1)
    return jnp.where(n < REL_MAX_EXACT, n, large)


def _bias_lookup(rb_ref, head, bucket):
    out = jnp.full(bucket.shape, rb_ref[head, 0], F32)
    for kk in range(1, REL_BUCKETS):
        out = jnp.where(bucket >= kk, rb_ref[head, kk], out)
    return out


N_NEAR = 8


def _bias_tiles_kernel(rb_ref, o_ref):
    h = pl.program_id(0)
    row = lax.broadcasted_iota(jnp.int32, (QT, QT), 0)
    col = lax.broadcasted_iota(jnp.int32, (QT, QT), 1)
    for d in range(N_NEAR):
        bucket = _rel_bucket(d * QT + col - row)
        o_ref[0, d] = _bias_lookup(rb_ref, h, bucket) - rb_ref[h, REL_BUCKETS - 1]


def _bias_tiles(rel_bias):
    return pl.pallas_call(
        _bias_tiles_kernel,
        grid=(NSA_HEADS,),
        in_specs=[pl.BlockSpec(memory_space=pltpu.SMEM)],
        out_specs=pl.BlockSpec((1, N_NEAR, QT, QT), lambda h: (h, 0, 0, 0)),
        out_shape=jax.ShapeDtypeStruct((NSA_HEADS, N_NEAR, QT, QT), F32),
        compiler_params=_cparams("parallel"),
        name="nsa_bias_tiles",
    )(rel_bias)


def _compress_kernel(a_ref, w1_ref, b1_ref, w2_ref, pe_ref, o_ref):
    a = a_ref[0, 0, 0]
    ncp = a.shape[0]
    half = CMP_STRIDE * HEAD_DIM
    h1 = _dot(a, w1_ref[0, :half, :])
    h2 = _dot(a, w1_ref[0, half:, :])
    h2 = pltpu.roll(h2, ncp - 1, 0)
    rowi = lax.broadcasted_iota(jnp.int32, h2.shape, 0)
    h2 = jnp.where(rowi < ncp - 1, h2, 0.0)
    pev = _dot(pe_ref[0], w1_ref[0])[0:1]
    hid = h1 + h2 + pev + b1_ref[0]
    act = hid * _sigmoid(hid)
    o_ref[0, 0, 0] = _dot(act.astype(BF16), w2_ref[0]).astype(o_ref.dtype)


def _compress(a, w1, b1, w2, pe):
    _, b, g, ncp, kk = a.shape
    hid = w1.shape[2]
    pe8 = jnp.pad(pe.reshape(2, 1, CMP_BLOCK * HEAD_DIM), ((0, 0), (0, 7), (0, 0))).astype(BF16)
    return pl.pallas_call(
        _compress_kernel,
        grid=(2, b, g),
        in_specs=[pl.BlockSpec((1, 1, 1, ncp, kk), lambda t, bi, gi: (t, bi, gi, 0, 0)),
                  pl.BlockSpec((1, 2 * kk, hid), lambda t, bi, gi: (t, 0, 0)),
                  pl.BlockSpec((1, 1, hid), lambda t, bi, gi: (t, 0, 0)),
                  pl.BlockSpec((1, hid, HEAD_DIM), lambda t, bi, gi: (t, 0, 0)),
                  pl.BlockSpec((1, 8, 2 * kk), lambda t, bi, gi: (t, 0, 0))],
        out_specs=pl.BlockSpec((1, 1, 1, ncp, HEAD_DIM), lambda t, bi, gi: (t, bi, gi, 0, 0)),
        out_shape=jax.ShapeDtypeStruct((2, b, g, ncp, HEAD_DIM), BF16),
        compiler_params=_cparams("parallel", "parallel", "parallel"),
        name="nsa_compress",
    )(a, w1, b1.reshape(2, 1, hid), w2, pe8)


BUCKET_CONST_FROM = 800


def _cattn_kernel(rb_ref, qt_ref, k_ref, vt_ref, ov_ref, oc_ref, sel_ref, bias_ref):
    g = pl.program_id(0)
    i = pl.program_id(1)
    nb = k_ref.shape[0]
    ncp = k_ref.shape[2]
    q0 = i * QT
    for r in range(ncp // QT):
        rows = slice(r * QT, (r + 1) * QT)
        min_rel = q0 - (CMP_STRIDE * (r * QT + QT - 1) + CMP_BLOCK - 1)
        max_rel = q0 + QT - 1 - (CMP_STRIDE * r * QT + CMP_BLOCK - 1)

        @pl.when((min_rel < BUCKET_CONST_FROM) & (max_rel >= 0))
        def _(r=r, rows=rows):
            t_r = q0 + lax.broadcasted_iota(jnp.int32, (QT, QT), 1)
            cend_r = (r * QT + lax.broadcasted_iota(jnp.int32, (QT, QT), 0)) * CMP_STRIDE + (CMP_BLOCK - 1)
            bucket = _rel_bucket(t_r - cend_r)
            for h in range(NSA_HG):
                bias_ref[h, rows, :] = jnp.where(cend_r <= t_r, _bias_lookup(rb_ref, g * NSA_HG + h, bucket), NEG)

        @pl.when(min_rel >= BUCKET_CONST_FROM)
        def _(rows=rows):
            for h in range(NSA_HG):
                bias_ref[h, rows, :] = jnp.full((QT, QT), rb_ref[g * NSA_HG + h, REL_BUCKETS - 1], F32)

        @pl.when(max_rel < 0)
        def _(rows=rows):
            for h in range(NSA_HG):
                bias_ref[h, rows, :] = jnp.full((QT, QT), NEG, F32)

    anyvalid = q0 + lax.broadcasted_iota(jnp.int32, (1, QT), 1) >= CMP_BLOCK - 1
    qks = [_dot(k_ref[bi, 0], qt_ref[bi, 0, 0]) for bi in range(nb)]
    pcsums = []
    for bi in range(nb):
        vt = vt_ref[bi, 0]
        pcsum = jnp.zeros((ncp, QT), F32)
        ocs = []
        for h in range(NSA_HG):
            lg = qks[bi][:, h * QT:(h + 1) * QT] + bias_ref[h]
            m = jnp.max(lg, axis=0, keepdims=True)
            e = jnp.exp(lg - m)
            p = e * jnp.where(anyvalid, 1.0 / jnp.sum(e, axis=0, keepdims=True), 0.0)
            pcsum = pcsum + p
            ocs.append(_dot(vt, p.astype(BF16)))
        oc_ref[bi, 0, 0] = jnp.concatenate(ocs, axis=1).astype(oc_ref.dtype)
        pcsums.append(pcsum)

    imps = []
    for pcsum in pcsums:
        hi = pcsum.astype(BF16)
        lo = (pcsum - hi.astype(F32)).astype(BF16)
        imps.append(_dot(ov_ref[...], hi) + _dot(ov_ref[...], lo))
    jrow = lax.broadcasted_iota(jnp.int32, (LANES, QT), 0)
    tq = q0 + lax.broadcasted_iota(jnp.int32, (LANES, QT), 1)
    cur = jnp.right_shift(tq, SEL_BLOCK.bit_length() - 1)
    forced = (jrow == 0) | (jrow == cur) | (jrow == cur - 1)
    jf = jrow.astype(F32)
    for bi, imp in enumerate(imps):
        val = jnp.where(forced, FORCE, jnp.where(jrow * SEL_BLOCK <= tq, imp, -1.0))
        sel = jnp.zeros((LANES, QT), F32)
        for _ in range(N_SEL):
            m = jnp.max(val, axis=0, keepdims=True)
            jm = jnp.min(jnp.where(val == m, jf, 1e9), axis=0, keepdims=True)
            hit = jf == jm
            sel = jnp.where(hit, 1.0, sel)
            val = jnp.where(hit, M_INIT, val)
        sel_ref[bi, 0, 0] = ((sel - 1.0) * SEL_MASK).astype(sel_ref.dtype)


def _overlap_t(ncp):
    ci = (np.arange(ncp) * CMP_STRIDE)[None, :]
    sj = (np.arange(LANES) * SEL_BLOCK)[:, None]
    return jnp.asarray(((ci < sj + SEL_BLOCK) & (ci + CMP_BLOCK > sj)).astype(np.float32), dtype=BF16)


def _cattn(rel_bias, qt, kcmp, vcmp_t):
    b, g, nq, d, cols = qt.shape
    ncp = kcmp.shape[2]
    hg = NSA_HG
    q_spec = pl.BlockSpec((b, 1, 1, d, cols), lambda gi, i: (0, gi, i, 0, 0))
    return pl.pallas_call(
        _cattn_kernel,
        grid=(g, nq),
        in_specs=[pl.BlockSpec(memory_space=pltpu.SMEM),
                  q_spec,
                  pl.BlockSpec((b, 1, ncp, d), lambda gi, i: (0, gi, 0, 0)),
                  pl.BlockSpec((b, 1, d, ncp), lambda gi, i: (0, gi, 0, 0)),
                  pl.BlockSpec((LANES, ncp), lambda gi, i: (0, 0))],
        out_specs=[q_spec,
                   pl.BlockSpec((b, 1, 1, LANES, QT), lambda gi, i: (0, gi, i, 0, 0))],
        out_shape=[jax.ShapeDtypeStruct((b, g, nq, d, cols), BF16),
                   jax.ShapeDtypeStruct((b, g, nq, LANES, QT), BF16)],
        scratch_shapes=[pltpu.VMEM((hg, ncp, QT), F32)],
        compiler_params=_cparams("parallel", "parallel"),
        name="nsa_cmp_attn",
    )(rel_bias, qt, kcmp, vcmp_t, _overlap_t(ncp))


VROWS = 80


def _sattn_kernel(qt_ref, sel_ref, ks_ref, vs_ref, kw_ref, vw_ref, tt_ref, oc_ref, gt_ref, o_ref,
                  m_ref, acc_ref, mw_ref, accw_ref, s0_ref, s1_ref):
    i = pl.program_id(2)
    cols = NSA_HG * QT
    qt = qt_ref[0, 0, 0]
    qa = jnp.concatenate([qt, jnp.concatenate([sel_ref[0, 0, 0]] * NSA_HG, axis=1)], axis=0)
    krow = lax.broadcasted_iota(jnp.int32, (QT, cols), 0)
    qcol = lax.broadcasted_iota(jnp.int32, (QT, cols), 1) & (QT - 1)
    sel_state = (m_ref, acc_ref)
    win_state = (mw_ref, accw_ref)

    for mr, ar in (sel_state, win_state):
        mr[...] = jnp.full(mr.shape, M_INIT, F32)
        ar[...] = jnp.zeros(ar.shape, F32)

    def update_state(state, *tiles):
        mr, ar = state
        m = mr[...]
        acc = ar[...]
        for s, vt in tiles:
            m_new = jnp.maximum(m, jnp.max(s, axis=0, keepdims=True))
            p = jnp.exp(s - m_new[0:1])
            acc = acc * jnp.exp(m - m_new)[0:1] + _dot(vt, p.astype(BF16))
            m = m_new
        ar[...] = acc
        mr[...] = m

    def update(*tiles):
        update_state(sel_state, *tiles)

    def near_bias(d):
        return jnp.concatenate([tt_ref[h, d] for h in range(NSA_HG)], axis=1)

    def finish(state):
        acc = state[1][...]
        return acc[:HEAD_DIM] / acc[HEAD_DIM:HEAD_DIM + 1]

    def sel_tile(kt, d, size=QT):
        k0 = pl.multiple_of(kt * QT, QT)
        s = _dot(ks_ref[0, 0, pl.ds(k0, size), :], qa)
        if d is not None:
            s = s + near_bias(d)
        return s, vs_ref[0, 0, :, pl.ds(k0, size)]

    def group(k_ref, v_ref, q_op, kt0, ds, first_gt, last_le):
        n = len(ds)
        k0 = pl.multiple_of(kt0 * QT, QT)
        s = _dot(k_ref[0, 0, pl.ds(k0, n * QT), :], q_op)
        parts = []
        for t, d in enumerate(ds):
            blk = s[t * QT:(t + 1) * QT] + near_bias(d)
            if t == 0 and first_gt:
                blk = jnp.where(krow > qcol, blk, NEG)
            if t == n - 1 and last_le:
                blk = jnp.where(krow <= qcol, blk, NEG)
            parts.append(blk)
        return jnp.concatenate(parts, axis=0), v_ref[0, 0, :, pl.ds(k0, n * QT)]

    big = (N_NEAR // 2) * QT
    nwin = WINDOW // QT

    def window_tile():
        return group(kw_ref, vw_ref, qt, i - nwin, list(range(nwin, -1, -1)), True, True)

    @pl.when(i >= N_NEAR - 1)
    def _():
        n_far = i - (N_NEAR - 1)
        n_chunks = (n_far + N_NEAR - 1) // N_NEAR

        def start(c):
            return jnp.maximum(n_far - N_NEAR * (n_chunks - c), 0)

        def logits(c, dst_ref):
            for hf in range(2):
                k0 = pl.multiple_of(start(c) * QT + hf * big, QT)
                dst_ref[hf] = _dot(ks_ref[0, 0, pl.ds(k0, big), :], qa)

        def values(c, hf):
            k0 = pl.multiple_of(start(c) * QT + hf * big, QT)
            return vs_ref[0, 0, :, pl.ds(k0, big)]

        key_lane = lax.broadcasted_iota(jnp.int32, (VROWS, big), 1)

        def far_step(c, src_ref, dst_ref):
            logits(c + 1, dst_ref)
            own = (start(c + 1) - start(c)) * QT
            tiles = []
            for hf in range(2):
                vt = values(c, hf)
                tiles.append((src_ref[hf], jnp.where(key_lane < own - hf * big, vt, jnp.zeros_like(vt))))
            update(*tiles)

        def near_step(src_ref):
            win = window_tile()
            tiles = []
            for hf in range(2):
                parts = []
                for t in range(N_NEAR // 2):
                    d = N_NEAR - 1 - (hf * (N_NEAR // 2) + t)
                    blk = src_ref[hf, t * QT:(t + 1) * QT, :] + near_bias(d)
                    if d == 0:
                        blk = jnp.where(krow <= qcol, blk, NEG)
                    parts.append(blk)
                tiles.append((jnp.concatenate(parts, axis=0), values(n_chunks, hf)))
            update(*tiles)
            update_state(win_state, win)

        logits(0, s0_ref)

        def far_body(j, carry):
            far_step(2 * j, s0_ref, s1_ref)
            far_step(2 * j + 1, s1_ref, s0_ref)
            return carry

        lax.fori_loop(0, n_chunks // 2, far_body, 0)

        @pl.when(n_chunks % 2 == 1)
        def _():
            far_step(n_chunks - 1, s0_ref, s1_ref)
            near_step(s1_ref)

        @pl.when(n_chunks % 2 == 0)
        def _():
            near_step(s0_ref)

    def near_body(kt, carry):
        update(sel_tile(kt, i - kt))
        return carry

    @pl.when(i < N_NEAR - 1)
    def _():
        lax.fori_loop(0, i, near_body, 0)
        s, vt = sel_tile(i, 0)
        update((jnp.where(krow <= qcol, s, NEG), vt))

        @pl.when(i >= nwin)
        def _():
            update_state(win_state, window_tile())

        @pl.when(i < nwin)
        def _():
            for d in range(nwin - 1, -1, -1):
                @pl.when(i - d >= 0)
                def _(d=d):
                    update_state(win_state, group(kw_ref, vw_ref, qt, i - d, [d], False, d == 0))

    osl = finish(sel_state)
    ow = finish(win_state)

    gt = _sigmoid(gt_ref[0, 0])
    gate = lambda br: jnp.concatenate([gt[3 * h + br:3 * h + br + 1] for h in range(NSA_HG)], axis=1)
    o = gate(0) * oc_ref[0, 0, 0].astype(F32) + gate(1) * osl + gate(2) * ow
    o_ref[0, 0, 0] = o.astype(o_ref.dtype)


def _sattn(qt, selb, ksa, vst, kw, vwt, tt, oct, gates_t):
    b, g, nq, d, cols = qt.shape
    s = kw.shape[2]
    hg = NSA_HG
    k_spec = lambda width: pl.BlockSpec((1, 1, s, width), lambda bi, gi, i: (bi, gi, 0, 0))
    v_spec = pl.BlockSpec((1, 1, VROWS, s), lambda bi, gi, i: (bi, gi, 0, 0))
    q_spec = pl.BlockSpec((1, 1, 1, d, cols), lambda bi, gi, i: (bi, gi, i, 0, 0))
    return pl.pallas_call(
        _sattn_kernel,
        grid=(b, g, nq),
        in_specs=[q_spec,
                  pl.BlockSpec((1, 1, 1, LANES, QT), lambda bi, gi, i: (bi, gi, i, 0, 0)),
                  k_spec(ksa.shape[3]), v_spec, k_spec(d), v_spec,
                  pl.BlockSpec((hg, N_NEAR, QT, QT), lambda bi, gi, i: (gi, 0, 0, 0)),
                  q_spec,
                  pl.BlockSpec((1, 1, 16, QT), lambda bi, gi, i: (bi, gi, 0, i))],
        out_specs=q_spec,
        out_shape=jax.ShapeDtypeStruct((b, g, nq, d, cols), BF16),
        scratch_shapes=[pltpu.VMEM((8, cols), F32),
                        pltpu.VMEM((VROWS, cols), F32),
                        pltpu.VMEM((8, cols), F32),
                        pltpu.VMEM((VROWS, cols), F32),
                        pltpu.VMEM((2, (N_NEAR // 2) * QT, cols), F32),
                        pltpu.VMEM((2, (N_NEAR // 2) * QT, cols), F32)],
        compiler_params=_cparams("parallel", "parallel", "arbitrary"),
        name="nsa_sel_win_attn",
    )(qt, selb, ksa, vst, kw, vwt, tt, oct, gates_t)


def _heads(t, h):
    b, s, _ = t.shape
    return t.reshape(b, s, h, HEAD_DIM).transpose(0, 2, 1, 3)


def _mixer_ab(x, g, scale, shift, w_in, conv_w, conv_b, dt_bias, a_log, d_skip, ssm_norm):
    b, s, d = x.shape
    o3 = 3 * SB_WIDTH
    qs = HEAD_DIM ** -0.5
    w_q, w_k, w_v = w_in[:, :SB_WIDTH] * qs, w_in[:, SB_WIDTH:2 * SB_WIDTH], w_in[:, 2 * SB_WIDTH:o3]
    w_z = w_in[:, o3:o3 + SSM_INNER]
    w_xbc = w_in[:, o3 + SSM_INNER:o3 + SSM_INNER + SSM_CONV_DIM]
    w_dt = w_in[:, o3 + SSM_INNER + SSM_CONV_DIM:]
    w_main = jnp.concatenate([w_xbc, w_z, w_q, w_k, w_v], axis=1).astype(BF16)
    w_dt = jnp.pad(w_dt, ((0, 0), (0, LANES - SSM_HEADS))).astype(BF16)
    proj, dt_raw = _norm_proj(x, g, scale, shift, w_main, w_dt, 512, "ab_in_proj")
    c0 = SSM_CONV_DIM + SSM_INNER
    o_sb = _sb_attention(proj, c0, c0 + SB_WIDTH, c0 + 2 * SB_WIDTH)
    y = _ssd(proj, dt_raw, conv_w, conv_b, dt_bias, a_log, d_skip, ssm_norm)
    return o_sb, y


def _mixer_nsa(x, g, scale, shift, tt, rel_bias, w_in, cmp_w1, cmp_b1, cmp_w2, cmp_pe):
    b, s, d = x.shape
    gq, hg = NSA_GROUPS, NSA_HG
    qs = HEAD_DIM ** -0.5
    n_main = NSA_Q + 6 * NSA_KV
    w_main = jnp.concatenate([w_in[:, :NSA_Q] * qs, w_in[:, NSA_Q:n_main]], axis=1).astype(BF16)
    n_gate = NSA_HEADS * N_BRANCH
    w_gate = jnp.pad(w_in[:, n_main:], ((0, 0), (0, LANES - n_gate))).astype(BF16)
    proj, gates = _norm_proj(x, g, scale, shift, w_main, w_gate, 512, "nsa_in_proj")
    gates_t = gates[..., :n_gate].reshape(b, s, gq, hg * N_BRANCH).transpose(0, 2, 3, 1)
    gates_t = jnp.pad(gates_t, ((0, 0), (0, 0), (0, 16 - hg * N_BRANCH), (0, 0)))
    nq = s // QT
    q5 = proj[..., :NSA_Q].reshape(b, nq, QT, gq, hg, HEAD_DIM)
    qt = q5.transpose(0, 3, 1, 5, 4, 2).reshape(b, gq, nq, HEAD_DIM, hg * QT)
    part = lambda j: proj[..., NSA_Q + j * NSA_KV:NSA_Q + (j + 1) * NSA_KV]
    nc = s // CMP_STRIDE
    kvc = jnp.stack([part(0), part(1)])
    a = kvc.reshape(2, b, nc, CMP_STRIDE, gq, HEAD_DIM).transpose(0, 1, 4, 2, 3, 5)
    a = a.reshape(2, b, gq, nc, CMP_STRIDE * HEAD_DIM)
    cmp = _compress(a, cmp_w1.astype(BF16), cmp_b1, cmp_w2.astype(BF16), cmp_pe)
    oct, selb = _cattn(rel_bias, qt, cmp[0], cmp[1].transpose(0, 1, 3, 2))
    grp = lambda t: t.reshape(b, s, gq, HEAD_DIM).transpose(0, 2, 1, 3)
    onehot = (jnp.arange(s)[:, None] // SEL_BLOCK == jnp.arange(LANES)[None, :]).astype(BF16)
    ksa = jnp.concatenate([grp(part(2)), jnp.broadcast_to(onehot, (b, gq, s, LANES))], axis=-1)

    def values_t(t):
        vt = t.reshape(b, s, gq, HEAD_DIM).transpose(0, 2, 3, 1)
        ones = jnp.ones((b, gq, 1, s), BF16)
        return jnp.concatenate([vt, ones, jnp.zeros((b, gq, VROWS - HEAD_DIM - 1, s), BF16)], axis=2)

    ot = _sattn(qt, selb, ksa, values_t(part(3)), grp(part(4)), values_t(part(5)), tt, oct, gates_t)
    o = ot.reshape(b, gq, nq, HEAD_DIM, hg, QT).transpose(0, 2, 5, 1, 4, 3)
    return o.reshape(b, s, NSA_Q)


def kernel(x, c, rel_bias, ada_w, ada_b, norm_g, ab_w_in, ab_conv_w, ab_conv_b, ab_dt_bias, ab_a_log,
           ab_d_skip, ab_ssm_norm, ab_w_out, nsa_w_in, nsa_cmp_w1, nsa_cmp_b1, nsa_cmp_w2, nsa_cmp_pe,
           nsa_w_out, ffn_w_up, ffn_conv_w, ffn_conv_b, ffn_w_down):
    depth = ada_w.shape[0]
    d = x.shape[-1]
    assert x.shape[1] % (CMP_STRIDE * QT) == 0 and x.shape[1] // SEL_BLOCK <= LANES
    mod = _modulation(c, ada_w, ada_b)
    tt = _bias_tiles(rel_bias) if depth > 1 else None
    for l in range(depth):
        shift1, scale1, gate1, shift2, scale2, gate2 = [mod[l, :, j * d:(j + 1) * d] for j in range(6)]
        if l % 2 == 0:
            e = l // 2
            o_sb, y = _mixer_ab(x, norm_g[l, 0], scale1, shift1, ab_w_in[e], ab_conv_w[e], ab_conv_b[e],
                                ab_dt_bias[e], ab_a_log[e], ab_d_skip[e], ab_ssm_norm[e])
            w_out = ab_w_out[e].astype(BF16)
            x = _out_proj([o_sb, y], [w_out[:SB_WIDTH], w_out[SB_WIDTH:]], norm_g[l, 1], gate1, x, "ab_out_proj")
        else:
            o = l // 2
            a = _mixer_nsa(x, norm_g[l, 0], scale1, shift1, tt, rel_bias, nsa_w_in[o], nsa_cmp_w1[o],
                           nsa_cmp_b1[o], nsa_cmp_w2[o], nsa_cmp_pe[o])
            x = _out_proj([a], [nsa_w_out[o].astype(BF16)], norm_g[l, 1], gate1, x, "nsa_out_proj")
        a = _ffn_up(x, norm_g[l, 2], scale2, shift2, ffn_w_up[l].astype(BF16), ffn_conv_w[l], ffn_conv_b[l])
        x = _out_proj([a], [ffn_w_down[l].astype(BF16)], norm_g[l, 3], gate2, x, "ffn_down_proj")
    return x
```

```python
import functools
import math

import numpy as np
import jax
import jax.numpy as jnp
from jax import lax
from jax.experimental import pallas as pl
from jax.experimental.pallas import tpu as pltpu

BF16 = jnp.bfloat16
F32 = jnp.float32

SB_HEADS = 8
HEAD_DIM = 64
SB_WIDTH = SB_HEADS * HEAD_DIM
SSM_HEADS = 16
SSM_INNER = SSM_HEADS * HEAD_DIM
SSM_GROUPS = 4
SSM_STATE = 128
SSM_CONV = 4
SSM_CHUNK = 128
SSM_CONV_DIM = SSM_INNER + 2 * SSM_GROUPS * SSM_STATE
NSA_HEADS = 16
NSA_GROUPS = 4
NSA_HG = NSA_HEADS // NSA_GROUPS
NSA_Q = NSA_HEADS * HEAD_DIM
NSA_KV = NSA_GROUPS * HEAD_DIM
CMP_BLOCK = 32
CMP_STRIDE = 16
SEL_BLOCK = 64
N_SEL = 16
WINDOW = 512
N_BRANCH = 3
REL_BUCKETS = 32
REL_MAX_EXACT = 16
REL_MAX_DIST = 1024
FFN_CONV = 3
EPS = 1e-6
NEG = -1e30
FORCE = 1e4
SEL_MASK = 2.0 ** 100
M_INIT = -3.0e38

QT = 128
LANES = 128
VMEM_LIMIT = 48 * 1024 * 1024


def _cparams(*sem):
    return pltpu.CompilerParams(dimension_semantics=sem, vmem_limit_bytes=VMEM_LIMIT)


def _sigmoid(x):
    return 1.0 / (1.0 + jnp.exp(-x))


def _dot(a, b):
    return jnp.dot(a, b, preferred_element_type=F32)


def _dot_nt(a, b):
    return lax.dot_general(a, b, (((1,), (1,)), ((), ())), preferred_element_type=F32)


def _rms(y, g):
    ms = jnp.mean(y * y, axis=-1, keepdims=True)
    return y * lax.rsqrt(ms + EPS) * g


def _mod_kernel(c_ref, w_ref, b_ref, o_ref):
    c = c_ref[...]
    s = c * _sigmoid(c)
    o_ref[0] = _dot(s.astype(BF16), w_ref[0].astype(BF16)) + b_ref[0]


def _modulation(c, ada_w, ada_b):
    depth, d, n = ada_w.shape
    b = c.shape[0]
    bp = 8
    tn = 768
    cp = jnp.pad(c, ((0, bp - b), (0, 0)))
    out = pl.pallas_call(
        _mod_kernel,
        grid=(depth, n // tn),
        in_specs=[pl.BlockSpec((bp, d), lambda l, j: (0, 0)),
                  pl.BlockSpec((1, d, tn), lambda l, j: (l, 0, j)),
                  pl.BlockSpec((1, 1, tn), lambda l, j: (l, 0, j))],
        out_specs=pl.BlockSpec((1, bp, tn), lambda l, j: (l, 0, j)),
        out_shape=jax.ShapeDtypeStruct((depth, bp, n), F32),
        compiler_params=_cparams("parallel", "parallel"),
        name="adaln_mod",
    )(cp, ada_w, ada_b.reshape(depth, 1, n))
    return out[:, :b]


def _norm_proj_kernel(x_ref, g_ref, sc_ref, sh_ref, w_ref, w2_ref, o_ref, o2_ref, *, tn):
    y = _rms(x_ref[0], g_ref[...])
    h = (y * (1.0 + sc_ref[0]) + sh_ref[0]).astype(BF16)
    for j in range(w_ref.shape[1] // tn):
        o_ref[0, :, j * tn:(j + 1) * tn] = _dot(h, w_ref[:, j * tn:(j + 1) * tn]).astype(o_ref.dtype)
    o2_ref[0] = _dot(h, w2_ref[...])


def _norm_proj(x, g, scale, shift, w, w_f32, tn, name):
    b, s, d = x.shape
    n = w.shape[1]
    tm = min(512, s)
    return pl.pallas_call(
        functools.partial(_norm_proj_kernel, tn=tn),
        grid=(b, s // tm),
        in_specs=[pl.BlockSpec((1, tm, d), lambda bi, i: (bi, i, 0)),
                  pl.BlockSpec((1, d), lambda bi, i: (0, 0)),
                  pl.BlockSpec((1, 1, d), lambda bi, i: (bi, 0, 0)),
                  pl.BlockSpec((1, 1, d), lambda bi, i: (bi, 0, 0)),
                  pl.BlockSpec((d, n), lambda bi, i: (0, 0)),
                  pl.BlockSpec((d, LANES), lambda bi, i: (0, 0))],
        out_specs=[pl.BlockSpec((1, tm, n), lambda bi, i: (bi, i, 0)),
                   pl.BlockSpec((1, tm, LANES), lambda bi, i: (bi, i, 0))],
        out_shape=[jax.ShapeDtypeStruct((b, s, n), BF16), jax.ShapeDtypeStruct((b, s, LANES), F32)],
        compiler_params=_cparams("parallel", "parallel"),
        name=name,
    )(x, g.reshape(1, d), scale.reshape(b, 1, d), shift.reshape(b, 1, d), w, w_f32)


def _out_proj_kernel(*refs, n_in):
    a_refs = refs[:n_in]
    w_refs = refs[n_in:2 * n_in]
    g_ref, gate_ref, x_ref, o_ref = refs[2 * n_in:]
    acc = _dot(a_refs[0][0], w_refs[0][...])
    for a_ref, w_ref in zip(a_refs[1:], w_refs[1:]):
        acc = acc + _dot(a_ref[0], w_ref[...])
    o_ref[0] = x_ref[0] + gate_ref[0] * _rms(acc, g_ref[...])


def _out_proj(acts, ws, g, gate, x, name):
    b, s, d = x.shape
    tm = min(512, s)
    n_in = len(acts)
    in_specs = [pl.BlockSpec((1, tm, a.shape[2]), lambda bi, i: (bi, i, 0)) for a in acts]
    in_specs += [pl.BlockSpec(w.shape, lambda bi, i: (0, 0)) for w in ws]
    in_specs += [pl.BlockSpec((1, d), lambda bi, i: (0, 0)),
                 pl.BlockSpec((1, 1, d), lambda bi, i: (bi, 0, 0)),
                 pl.BlockSpec((1, tm, d), lambda bi, i: (bi, i, 0))]
    return pl.pallas_call(
        functools.partial(_out_proj_kernel, n_in=n_in),
        grid=(b, s // tm),
        in_specs=in_specs,
        out_specs=pl.BlockSpec((1, tm, d), lambda bi, i: (bi, i, 0)),
        out_shape=jax.ShapeDtypeStruct((b, s, d), F32),
        compiler_params=_cparams("parallel", "parallel"),
        name=name,
    )(*acts, *ws, g.reshape(1, d), gate.reshape(b, 1, d), x)


def _shift_rows(cur, prev, k):
    sc = pltpu.roll(cur, k, 0)
    sp = pltpu.roll(prev, k, 0)
    row8 = lax.broadcasted_iota(jnp.int32, sp.shape, 0)
    head = jnp.where(row8 < k, sp, sc[:8])
    return jnp.concatenate([head, sc[8:]], axis=0)


def _gelu_tanh_x2(x):
    c = math.sqrt(2.0 / math.pi)
    return x * (1.0 + jnp.tanh(x * (c + (c * 0.044715) * (x * x))))


def _ffn_up_kernel(x_ref, xp_ref, g_ref, sc_ref, sh_ref, w_ref, cw_ref, cb_ref, o_ref, u_ref, *, dff, ck):
    def modulated(xv):
        return (_rms(xv, g_ref[...]) * (1.0 + sc_ref[0]) + sh_ref[0]).astype(BF16)

    tm = x_ref.shape[1]
    h = modulated(x_ref[0])
    hp = modulated(xp_ref[0])
    first = pl.program_id(1) == 0

    def conv(slot, col0):
        w_up = w_ref[:, col0:col0 + ck]
        cur = _dot(h, w_up)
        u_ref[slot, 0:8, :] = jnp.where(first, 0.0, _dot(hp, w_up))
        u_ref[slot, 8:8 + tm, :] = cur
        w = cw_ref[:, col0:col0 + ck]
        y = w[2:3] * cur + cb_ref[:, col0:col0 + ck]
        y = y + w[1:2] * u_ref[slot, 7:7 + tm, :]
        return y + w[0:1] * u_ref[slot, 6:6 + tm, :]

    for c in range(dff // ck):
        act = _gelu_tanh_x2(conv(0, c * ck)) * conv(1, dff + c * ck)
        o_ref[0, :, c * ck:(c + 1) * ck] = act.astype(o_ref.dtype)


def _ffn_up(x, g, scale, shift, w_up, conv_w, conv_b):
    b, s, d = x.shape
    c2 = w_up.shape[1]
    dff = c2 // 2
    tm = min(512, s)
    half = jnp.concatenate([jnp.ones((dff,), F32), jnp.full((dff,), 0.5, F32)])
    cw = jnp.pad(conv_w * half, ((0, 8 - conv_w.shape[0]), (0, 0)))
    conv_b = conv_b * half
    return pl.pallas_call(
        functools.partial(_ffn_up_kernel, dff=dff, ck=256),
        grid=(b, s // tm),
        in_specs=[pl.BlockSpec((1, tm, d), lambda bi, i: (bi, i, 0)),
                  pl.BlockSpec((1, 8, d), lambda bi, i: (bi, jnp.maximum(i * (tm // 8) - 1, 0), 0)),
                  pl.BlockSpec((1, d), lambda bi, i: (0, 0)),
                  pl.BlockSpec((1, 1, d), lambda bi, i: (bi, 0, 0)),
                  pl.BlockSpec((1, 1, d), lambda bi, i: (bi, 0, 0)),
                  pl.BlockSpec((d, c2), lambda bi, i: (0, 0)),
                  pl.BlockSpec((8, c2), lambda bi, i: (0, 0)),
                  pl.BlockSpec((1, c2), lambda bi, i: (0, 0))],
        out_specs=pl.BlockSpec((1, tm, dff), lambda bi, i: (bi, i, 0)),
        out_shape=jax.ShapeDtypeStruct((b, s, dff), BF16),
        scratch_shapes=[pltpu.VMEM((2, 8 + tm, 256), F32)],
        compiler_params=_cparams("parallel", "parallel"),
        name="ffn_up_conv_act",
    )(x, x, g.reshape(1, d), scale.reshape(b, 1, d), shift.reshape(b, 1, d), w_up, cw, conv_b.reshape(1, c2))


SBQ = 512


SB_TILES = 4
SB_DEAD = -104.0


def _sb_kernel(q_ref, k_ref, v_ref, u_ref, o_ref, acc_ref, cs_ref):
    i = pl.program_id(2)
    qp = q_ref[0]
    lane_lo = lax.broadcasted_iota(jnp.int32, (SBQ, LANES), 1) < HEAD_DIM
    zero = jnp.zeros_like(qp)
    qs = [jnp.where(lane_lo, qp, zero), jnp.where(lane_lo, zero, qp)]
    nsub = SBQ // QT
    diff = (lax.broadcasted_iota(jnp.int32, (SBQ, QT), 1) - lax.broadcasted_iota(jnp.int32, (SBQ, QT), 0))
    acc_ref[...] = jnp.zeros(acc_ref.shape, F32)
    cs_ref[...] = jnp.zeros(cs_ref.shape, F32)

    def tiles(kidxs, diag):
        offs = [pl.multiple_of(kidx * QT, QT) for kidx in kidxs]
        ks = [k_ref[0, pl.ds(k0, QT), :] for k0 in offs]
        zs = [[_dot_nt(q, k) for k in ks] for q in qs]
        css = [[], []]
        belows = [diff < i * SBQ - kidx * QT for kidx in kidxs] if diag else None
        for h in range(2):
            for t, z in enumerate(zs[h]):
                nl = jnp.maximum(z, 0.0) + jnp.log(1.0 + jnp.exp(-jnp.abs(z)))
                if diag:
                    nl = jnp.where(belows[t], nl, 0.0)
                hi = nl.astype(BF16)
                lo = (nl - hi.astype(F32)).astype(BF16)
                css[h].append(_dot(jnp.concatenate([hi, lo], axis=1), u_ref[...]))
        vs = [v_ref[0, pl.ds(k0, QT), :] for k0 in offs]
        for h in range(2):
            csum = cs_ref[h]
            acc = acc_ref[h]
            for t, (z, cs) in enumerate(zip(zs[h], css[h])):
                w = jnp.exp(z + cs[:, :QT] + csum)
                if diag:
                    w = jnp.where(belows[t], w, 0.0)
                acc = acc + _dot(w.astype(BF16), vs[t])
                csum = csum + cs[:, QT:]
            acc_ref[h] = acc
            cs_ref[h] = csum

    for r0 in range(nsub - 1, -1, -SB_TILES):
        tiles([i * nsub + r0 - r for r in range(SB_TILES)], True)

    n_trips = i * (nsub // SB_TILES)

    def cond(carry):
        it, cmax = carry
        return jnp.logical_and(it < n_trips, cmax > SB_DEAD)

    def body(carry):
        it, _ = carry
        tiles([i * nsub - 1 - it * SB_TILES - r for r in range(SB_TILES)], False)
        return it + 1, jnp.max(cs_ref[...])

    lax.while_loop(cond, body, (jnp.int32(0), jnp.max(cs_ref[...])))
    o_ref[0] = jnp.where(lane_lo, acc_ref[0], acc_ref[1]).astype(o_ref.dtype)


def _suffix_matrix():
    j = np.arange(2 * QT)[:, None] % QT
    s = np.arange(QT + LANES)[None, :]
    return jnp.asarray(-np.where(s < QT, j >= s, True).astype(np.float32), dtype=BF16)


def _sb_attention(proj, q_col, k_col, v_col):
    b, s, _ = proj.shape
    pairs = SB_HEADS // 2
    qb, kb, vb = q_col // LANES, k_col // LANES, v_col // LANES
    return pl.pallas_call(
        _sb_kernel,
        grid=(b, pairs, s // SBQ),
        in_specs=[pl.BlockSpec((1, SBQ, LANES), lambda bi, p, i: (bi, i, qb + p)),
                  pl.BlockSpec((1, s, LANES), lambda bi, p, i: (bi, 0, kb + p)),
                  pl.BlockSpec((1, s, LANES), lambda bi, p, i: (bi, 0, vb + p)),
                  pl.BlockSpec((2 * QT, QT + LANES), lambda bi, p, i: (0, 0))],
        out_specs=pl.BlockSpec((1, SBQ, LANES), lambda bi, p, i: (bi, i, p)),
        out_shape=jax.ShapeDtypeStruct((b, s, SB_WIDTH), BF16),
        scratch_shapes=[pltpu.VMEM((2, SBQ, LANES), F32), pltpu.VMEM((2, SBQ, LANES), F32)],
        compiler_params=_cparams("parallel", "parallel", "arbitrary"),
        name="sb_attention",
    )(proj, proj, proj, _suffix_matrix())


def _pair(f, p, lane_lo):
    return jnp.where(lane_lo, f[:, 2 * p:2 * p + 1], f[:, 2 * p + 1:2 * p + 2])


def _ssd_kernel(xr_ref, xp_ref, dt_ref, z_ref, cw_ref, cb_ref, dtb_ref, alog_ref, dsk_ref, nrm_ref,
                o_ref, st_ref, xc_ref, y_ref):
    ln = SSM_CHUNK
    c = pl.program_id(1)

    @pl.when(c == 0)
    def _():
        st_ref[...] = jnp.zeros(st_ref.shape, F32)

    first = c == 0
    ck = 256
    for j in range(SSM_CONV_DIM // ck):
        cur = xr_ref[0, :, j * ck:(j + 1) * ck].astype(F32)
        prev = xp_ref[0, :, j * ck:(j + 1) * ck].astype(F32)
        prev = jnp.where(first, 0.0, prev)
        w = cw_ref[:, j * ck:(j + 1) * ck]
        y = w[3:4] * cur + cb_ref[:, j * ck:(j + 1) * ck]
        for kk in range(1, SSM_CONV):
            y = y + w[3 - kk:4 - kk] * _shift_rows(cur, prev, kk)
        xc_ref[:, j * ck:(j + 1) * ck] = y * _sigmoid(y)

    dtr = dt_ref[0] + dtb_ref[...]
    dt = jnp.maximum(dtr, 0.0) + jnp.log(1.0 + jnp.exp(-jnp.abs(dtr)))
    a = dt * (-jnp.exp(alog_ref[...]))
    row = lax.broadcasted_iota(jnp.int32, (ln, ln), 0)
    col = lax.broadcasted_iota(jnp.int32, (ln, ln), 1)
    causal = col <= row
    acs = jnp.dot(causal.astype(F32), a, precision=lax.Precision.HIGHEST, preferred_element_type=F32)
    acs_t = acs.T
    eacs = jnp.exp(acs)
    dout = jnp.exp(acs[ln - 1:ln, :] - acs)
    lane_lo = lax.broadcasted_iota(jnp.int32, (ln, LANES), 1) < HEAD_DIM

    hg = SSM_HEADS // SSM_GROUPS
    for g in range(SSM_GROUPS):
        bm = xc_ref[:, SSM_INNER + g * SSM_STATE:SSM_INNER + (g + 1) * SSM_STATE]
        cm = xc_ref[:, SSM_INNER + (SSM_GROUPS + g) * SSM_STATE:SSM_INNER + (SSM_GROUPS + g + 1) * SSM_STATE]
        cmb = cm.astype(BF16)
        cb = _dot_nt(cmb, bm.astype(BF16))
        prev = st_ref[g]
        yoff = _dot(cmb, prev.astype(BF16))
        dox_parts = []
        cdec_parts = []
        for pp in range(hg // 2):
            p = g * (hg // 2) + pp
            xs = xc_ref[:, p * LANES:(p + 1) * LANES]
            xdt = xs * _pair(dt, p, lane_lo)
            ms = []
            for hh in (2 * p, 2 * p + 1):
                seg = acs[:, hh:hh + 1] - acs_t[hh:hh + 1, :]
                decay = jnp.exp(jnp.where(causal, seg, NEG))
                ms.append((cb * decay).astype(BF16))
            rhs = jnp.concatenate([jnp.where(lane_lo, xdt, 0.0), jnp.where(lane_lo, 0.0, xdt)], axis=0)
            ydiag = _dot(jnp.concatenate(ms, axis=1), rhs.astype(BF16))
            yo = yoff[:, pp * LANES:(pp + 1) * LANES] * _pair(eacs, p, lane_lo)
            y_ref[:, p * LANES:(p + 1) * LANES] = ydiag + yo + xs * dsk_ref[:, p * LANES:(p + 1) * LANES]
            dox_parts.append((xdt * _pair(dout, p, lane_lo)).astype(BF16))
            cdec_parts.append(_pair(eacs[ln - 1:ln, :], p, lane_lo[0:1]))
        states = _dot(bm.T.astype(BF16), jnp.concatenate(dox_parts, axis=1))
        st_ref[g] = prev * jnp.concatenate(cdec_parts, axis=1) + states

    zf = z_ref[0].astype(F32)
    yz = y_ref[...] * (zf * _sigmoid(zf))
    o_ref[0] = _rms(yz, nrm_ref[...]).astype(o_ref.dtype)


def _ssd(proj, dt_raw, conv_w, conv_b, dt_bias, a_log, d_skip, ssm_norm):
    b, s, _ = proj.shape
    ln = SSM_CHUNK
    cw = jnp.pad(conv_w, ((0, 8 - SSM_CONV), (0, 0)))
    pad = LANES - SSM_HEADS
    return pl.pallas_call(
        _ssd_kernel,
        grid=(b, s // ln),
        in_specs=[pl.BlockSpec((1, ln, SSM_CONV_DIM), lambda bi, c: (bi, c, 0)),
                  pl.BlockSpec((1, 8, SSM_CONV_DIM), lambda bi, c: (bi, jnp.maximum(c * (ln // 8) - 1, 0), 0)),
                  pl.BlockSpec((1, ln, LANES), lambda bi, c: (bi, c, 0)),
                  pl.BlockSpec((1, ln, SSM_INNER), lambda bi, c: (bi, c, SSM_CONV_DIM // SSM_INNER)),
                  pl.BlockSpec((8, SSM_CONV_DIM), lambda bi, c: (0, 0)),
                  pl.BlockSpec((1, SSM_CONV_DIM), lambda bi, c: (0, 0)),
                  pl.BlockSpec((1, LANES), lambda bi, c: (0, 0)),
                  pl.BlockSpec((1, LANES), lambda bi, c: (0, 0)),
                  pl.BlockSpec((1, SSM_INNER), lambda bi, c: (0, 0)),
                  pl.BlockSpec((1, SSM_INNER), lambda bi, c: (0, 0))],
        out_specs=pl.BlockSpec((1, ln, SSM_INNER), lambda bi, c: (bi, c, 0)),
        out_shape=jax.ShapeDtypeStruct((b, s, SSM_INNER), BF16),
        scratch_shapes=[pltpu.VMEM((SSM_GROUPS, SSM_STATE, SSM_INNER // SSM_GROUPS), F32),
                        pltpu.VMEM((ln, SSM_CONV_DIM), F32),
                        pltpu.VMEM((ln, SSM_INNER), F32)],
        compiler_params=_cparams("parallel", "arbitrary"),
        name="ssd",
    )(proj, proj, dt_raw, proj, cw, conv_b.reshape(1, -1),
      jnp.pad(dt_bias, (0, pad)).reshape(1, LANES), jnp.pad(a_log, (0, pad)).reshape(1, LANES),
      jnp.repeat(d_skip, HEAD_DIM).reshape(1, SSM_INNER), ssm_norm.reshape(1, SSM_INNER))


def _rel_bucket(n):
    n = jnp.maximum(n, 0)
    nf = jnp.maximum(n, 1).astype(F32)
    large = REL_MAX_EXACT + (jnp.log(nf / REL_MAX_EXACT) / math.log(REL_MAX_DIST / REL_MAX_EXACT)
                             * (REL_BUCKETS - REL_MAX_EXACT)).astype(jnp.int32)
    large = jnp.minimum(large, REL_BUCKETS - 1)
    return jnp.where(n < REL_MAX_EXACT, n, large)


def _bias_lookup(rb_ref, head, bucket):
    out = jnp.full(bucket.shape, rb_ref[head, 0], F32)
    for kk in range(1, REL_BUCKETS):
        out = jnp.where(bucket >= kk, rb_ref[head, kk], out)
    return out


N_NEAR = 8


def _bias_tiles_kernel(rb_ref, o_ref):
    h = pl.program_id(0)
    row = lax.broadcasted_iota(jnp.int32, (QT, QT), 0)
    col = lax.broadcasted_iota(jnp.int32, (QT, QT), 1)
    for d in range(N_NEAR):
        bucket = _rel_bucket(d * QT + col - row)
        o_ref[0, d] = _bias_lookup(rb_ref, h, bucket) - rb_ref[h, REL_BUCKETS - 1]


def _bias_tiles(rel_bias):
    return pl.pallas_call(
        _bias_tiles_kernel,
        grid=(NSA_HEADS,),
        in_specs=[pl.BlockSpec(memory_space=pltpu.SMEM)],
        out_specs=pl.BlockSpec((1, N_NEAR, QT, QT), lambda h: (h, 0, 0, 0)),
        out_shape=jax.ShapeDtypeStruct((NSA_HEADS, N_NEAR, QT, QT), F32),
        compiler_params=_cparams("parallel"),
        name="nsa_bias_tiles",
    )(rel_bias)


def _compress_kernel(a_ref, w1_ref, b1_ref, w2_ref, pe_ref, o_ref):
    a = a_ref[0, 0, 0]
    ncp = a.shape[0]
    half = CMP_STRIDE * HEAD_DIM
    h1 = _dot(a, w1_ref[0, :half, :])
    h2 = _dot(a, w1_ref[0, half:, :])
    h2 = pltpu.roll(h2, ncp - 1, 0)
    rowi = lax.broadcasted_iota(jnp.int32, h2.shape, 0)
    h2 = jnp.where(rowi < ncp - 1, h2, 0.0)
    pev = _dot(pe_ref[0], w1_ref[0])[0:1]
    hid = h1 + h2 + pev + b1_ref[0]
    act = hid * _sigmoid(hid)
    o_ref[0, 0, 0] = _dot(act.astype(BF16), w2_ref[0]).astype(o_ref.dtype)


def _compress(a, w1, b1, w2, pe):
    _, b, g, ncp, kk = a.shape
    hid = w1.shape[2]
    pe8 = jnp.pad(pe.reshape(2, 1, CMP_BLOCK * HEAD_DIM), ((0, 0), (0, 7), (0, 0))).astype(BF16)
    return pl.pallas_call(
        _compress_kernel,
        grid=(2, b, g),
        in_specs=[pl.BlockSpec((1, 1, 1, ncp, kk), lambda t, bi, gi: (t, bi, gi, 0, 0)),
                  pl.BlockSpec((1, 2 * kk, hid), lambda t, bi, gi: (t, 0, 0)),
                  pl.BlockSpec((1, 1, hid), lambda t, bi, gi: (t, 0, 0)),
                  pl.BlockSpec((1, hid, HEAD_DIM), lambda t, bi, gi: (t, 0, 0)),
                  pl.BlockSpec((1, 8, 2 * kk), lambda t, bi, gi: (t, 0, 0))],
        out_specs=pl.BlockSpec((1, 1, 1, ncp, HEAD_DIM), lambda t, bi, gi: (t, bi, gi, 0, 0)),
        out_shape=jax.ShapeDtypeStruct((2, b, g, ncp, HEAD_DIM), BF16),
        compiler_params=_cparams("parallel", "parallel", "parallel"),
        name="nsa_compress",
    )(a, w1, b1.reshape(2, 1, hid), w2, pe8)


BUCKET_CONST_FROM = 800


def _cattn_kernel(rb_ref, qt_ref, k_ref, vt_ref, ov_ref, oc_ref, sel_ref, bias_ref):
    g = pl.program_id(0)
    i = pl.program_id(1)
    nb = k_ref.shape[0]
    ncp = k_ref.shape[2]
    q0 = i * QT
    for r in range(ncp // QT):
        rows = slice(r * QT, (r + 1) * QT)
        min_rel = q0 - (CMP_STRIDE * (r * QT + QT - 1) + CMP_BLOCK - 1)
        max_rel = q0 + QT - 1 - (CMP_STRIDE * r * QT + CMP_BLOCK - 1)

        @pl.when((min_rel < BUCKET_CONST_FROM) & (max_rel >= 0))
        def _(r=r, rows=rows):
            t_r = q0 + lax.broadcasted_iota(jnp.int32, (QT, QT), 1)
            cend_r = (r * QT + lax.broadcasted_iota(jnp.int32, (QT, QT), 0)) * CMP_STRIDE + (CMP_BLOCK - 1)
            bucket = _rel_bucket(t_r - cend_r)
            for h in range(NSA_HG):
                bias_ref[h, rows, :] = jnp.where(cend_r <= t_r, _bias_lookup(rb_ref, g * NSA_HG + h, bucket), NEG)

        @pl.when(min_rel >= BUCKET_CONST_FROM)
        def _(rows=rows):
            for h in range(NSA_HG):
                bias_ref[h, rows, :] = jnp.full((QT, QT), rb_ref[g * NSA_HG + h, REL_BUCKETS - 1], F32)

        @pl.when(max_rel < 0)
        def _(rows=rows):
            for h in range(NSA_HG):
                bias_ref[h, rows, :] = jnp.full((QT, QT), NEG, F32)

    anyvalid = q0 + lax.broadcasted_iota(jnp.int32, (1, QT), 1) >= CMP_BLOCK - 1
    qks = [_dot(k_ref[bi, 0], qt_ref[bi, 0, 0]) for bi in range(nb)]
    pcsums = []
    for bi in range(nb):
        vt = vt_ref[bi, 0]
        pcsum = jnp.zeros((ncp, QT), F32)
        ocs = []
        for h in range(NSA_HG):
            lg = qks[bi][:, h * QT:(h + 1) * QT] + bias_ref[h]
            m = jnp.max(lg, axis=0, keepdims=True)
            e = jnp.exp(lg - m)
            p = e * jnp.where(anyvalid, 1.0 / jnp.sum(e, axis=0, keepdims=True), 0.0)
            pcsum = pcsum + p
            ocs.append(_dot(vt, p.astype(BF16)))
        oc_ref[bi, 0, 0] = jnp.concatenate(ocs, axis=1).astype(oc_ref.dtype)
        pcsums.append(pcsum)

    imps = []
    for pcsum in pcsums:
        hi = pcsum.astype(BF16)
        lo = (pcsum - hi.astype(F32)).astype(BF16)
        imps.append(_dot(ov_ref[...], hi) + _dot(ov_ref[...], lo))
    jrow = lax.broadcasted_iota(jnp.int32, (LANES, QT), 0)
    tq = q0 + lax.broadcasted_iota(jnp.int32, (LANES, QT), 1)
    cur = jnp.right_shift(tq, SEL_BLOCK.bit_length() - 1)
    forced = (jrow == 0) | (jrow == cur) | (jrow == cur - 1)
    jf = jrow.astype(F32)
    for bi, imp in enumerate(imps):
        val = jnp.where(forced, FORCE, jnp.where(jrow * SEL_BLOCK <= tq, imp, -1.0))
        sel = jnp.zeros((LANES, QT), F32)
        for _ in range(N_SEL):
            m = jnp.max(val, axis=0, keepdims=True)
            jm = jnp.min(jnp.where(val == m, jf, 1e9), axis=0, keepdims=True)
            hit = jf == jm
            sel = jnp.where(hit, 1.0, sel)
            val = jnp.where(hit, M_INIT, val)
        sel_ref[bi, 0, 0] = ((sel - 1.0) * SEL_MASK).astype(sel_ref.dtype)


def _overlap_t(ncp):
    ci = (np.arange(ncp) * CMP_STRIDE)[None, :]
    sj = (np.arange(LANES) * SEL_BLOCK)[:, None]
    return jnp.asarray(((ci < sj + SEL_BLOCK) & (ci + CMP_BLOCK > sj)).astype(np.float32), dtype=BF16)


def _cattn(rel_bias, qt, kcmp, vcmp_t):
    b, g, nq, d, cols = qt.shape
    ncp = kcmp.shape[2]
    hg = NSA_HG
    q_spec = pl.BlockSpec((b, 1, 1, d, cols), lambda gi, i: (0, gi, i, 0, 0))
    return pl.pallas_call(
        _cattn_kernel,
        grid=(g, nq),
        in_specs=[pl.BlockSpec(memory_space=pltpu.SMEM),
                  q_spec,
                  pl.BlockSpec((b, 1, ncp, d), lambda gi, i: (0, gi, 0, 0)),
                  pl.BlockSpec((b, 1, d, ncp), lambda gi, i: (0, gi, 0, 0)),
                  pl.BlockSpec((LANES, ncp), lambda gi, i: (0, 0))],
        out_specs=[q_spec,
                   pl.BlockSpec((b, 1, 1, LANES, QT), lambda gi, i: (0, gi, i, 0, 0))],
        out_shape=[jax.ShapeDtypeStruct((b, g, nq, d, cols), BF16),
                   jax.ShapeDtypeStruct((b, g, nq, LANES, QT), BF16)],
        scratch_shapes=[pltpu.VMEM((hg, ncp, QT), F32)],
        compiler_params=_cparams("parallel", "parallel"),
        name="nsa_cmp_attn",
    )(rel_bias, qt, kcmp, vcmp_t, _overlap_t(ncp))


VROWS = 80


def _sattn_kernel(qt_ref, sel_ref, ks_ref, vs_ref, kw_ref, vw_ref, tt_ref, oc_ref, gt_ref, o_ref,
                  m_ref, acc_ref, mw_ref, accw_ref, s0_ref, s1_ref):
    i = pl.program_id(2)
    cols = NSA_HG * QT
    qt = qt_ref[0, 0, 0]
    qa = jnp.concatenate([qt, jnp.concatenate([sel_ref[0, 0, 0]] * NSA_HG, axis=1)], axis=0)
    krow = lax.broadcasted_iota(jnp.int32, (QT, cols), 0)
    qcol = lax.broadcasted_iota(jnp.int32, (QT, cols), 1) & (QT - 1)
    sel_state = (m_ref, acc_ref)
    win_state = (mw_ref, accw_ref)

    for mr, ar in (sel_state, win_state):
        mr[...] = jnp.full(mr.shape, M_INIT, F32)
        ar[...] = jnp.zeros(ar.shape, F32)

    def update_state(state, *tiles):
        mr, ar = state
        m = mr[...]
        acc = ar[...]
        for s, vt in tiles:
            m_new = jnp.maximum(m, jnp.max(s, axis=0, keepdims=True))
            p = jnp.exp(s - m_new[0:1])
            acc = acc * jnp.exp(m - m_new)[0:1] + _dot(vt, p.astype(BF16))
            m = m_new
        ar[...] = acc
        mr[...] = m

    def update(*tiles):
        update_state(sel_state, *tiles)

    def near_bias(d):
        return jnp.concatenate([tt_ref[h, d] for h in range(NSA_HG)], axis=1)

    def finish(state):
        acc = state[1][...]
        return acc[:HEAD_DIM] / acc[HEAD_DIM:HEAD_DIM + 1]

    def sel_tile(kt, d, size=QT):
        k0 = pl.multiple_of(kt * QT, QT)
        s = _dot(ks_ref[0, 0, pl.ds(k0, size), :], qa)
        if d is not None:
            s = s + near_bias(d)
        return s, vs_ref[0, 0, :, pl.ds(k0, size)]

    def group(k_ref, v_ref, q_op, kt0, ds, first_gt, last_le):
        n = len(ds)
        k0 = pl.multiple_of(kt0 * QT, QT)
        s = _dot(k_ref[0, 0, pl.ds(k0, n * QT), :], q_op)
        parts = []
        for t, d in enumerate(ds):
            blk = s[t * QT:(t + 1) * QT] + near_bias(d)
            if t == 0 and first_gt:
                blk = jnp.where(krow > qcol, blk, NEG)
            if t == n - 1 and last_le:
                blk = jnp.where(krow <= qcol, blk, NEG)
            parts.append(blk)
        return jnp.concatenate(parts, axis=0), v_ref[0, 0, :, pl.ds(k0, n * QT)]

    big = (N_NEAR // 2) * QT
    nwin = WINDOW // QT

    def window_tile():
        return group(kw_ref, vw_ref, qt, i - nwin, list(range(nwin, -1, -1)), True, True)

    @pl.when(i >= N_NEAR - 1)
    def _():
        n_far = i - (N_NEAR - 1)
        n_chunks = (n_far + N_NEAR - 1) // N_NEAR

        def start(c):
            return jnp.maximum(n_far - N_NEAR * (n_chunks - c), 0)

        def logits(c, dst_ref):
            for hf in range(2):
                k0 = pl.multiple_of(start(c) * QT + hf * big, QT)
                dst_ref[hf] = _dot(ks_ref[0, 0, pl.ds(k0, big), :], qa)

        def values(c, hf):
            k0 = pl.multiple_of(start(c) * QT + hf * big, QT)
            return vs_ref[0, 0, :, pl.ds(k0, big)]

        key_lane = lax.broadcasted_iota(jnp.int32, (VROWS, big), 1)

        def far_step(c, src_ref, dst_ref):
            logits(c + 1, dst_ref)
            own = (start(c + 1) - start(c)) * QT
            tiles = []
            for hf in range(2):
                vt = values(c, hf)
                tiles.append((src_ref[hf], jnp.where(key_lane < own - hf * big, vt, jnp.zeros_like(vt))))
            update(*tiles)

        def near_step(src_ref):
            win = window_tile()
            tiles = []
            for hf in range(2):
                parts = []
                for t in range(N_NEAR // 2):
                    d = N_NEAR - 1 - (hf * (N_NEAR // 2) + t)
                    blk = src_ref[hf, t * QT:(t + 1) * QT, :] + near_bias(d)
                    if d == 0:
                        blk = jnp.where(krow <= qcol, blk, NEG)
                    parts.append(blk)
                tiles.append((jnp.concatenate(parts, axis=0), values(n_chunks, hf)))
            update(*tiles)
            update_state(win_state, win)

        logits(0, s0_ref)

        def far_body(j, carry):
            far_step(2 * j, s0_ref, s1_ref)
            far_step(2 * j + 1, s1_ref, s0_ref)
            return carry

        lax.fori_loop(0, n_chunks // 2, far_body, 0)

        @pl.when(n_chunks % 2 == 1)
        def _():
            far_step(n_chunks - 1, s0_ref, s1_ref)
            near_step(s1_ref)

        @pl.when(n_chunks % 2 == 0)
        def _():
            near_step(s0_ref)

    def near_body(kt, carry):
        update(sel_tile(kt, i - kt))
        return carry

    @pl.when(i < N_NEAR - 1)
    def _():
        lax.fori_loop(0, i, near_body, 0)
        s, vt = sel_tile(i, 0)
        update((jnp.where(krow <= qcol, s, NEG), vt))

        @pl.when(i >= nwin)
        def _():
            update_state(win_state, window_tile())

        @pl.when(i < nwin)
        def _():
            for d in range(nwin - 1, -1, -1):
                @pl.when(i - d >= 0)
                def _(d=d):
                    update_state(win_state, group(kw_ref, vw_ref, qt, i - d, [d], False, d == 0))

    osl = finish(sel_state)
    ow = finish(win_state)

    gt = _sigmoid(gt_ref[0, 0])
    gate = lambda br: jnp.concatenate([gt[3 * h + br:3 * h + br + 1] for h in range(NSA_HG)], axis=1)
    o = gate(0) * oc_ref[0, 0, 0].astype(F32) + gate(1) * osl + gate(2) * ow
    o_ref[0, 0, 0] = o.astype(o_ref.dtype)


def _sattn(qt, selb, ksa, vst, kw, vwt, tt, oct, gates_t):
    b, g, nq, d, cols = qt.shape
    s = kw.shape[2]
    hg = NSA_HG
    k_spec = lambda width: pl.BlockSpec((1, 1, s, width), lambda bi, gi, i: (bi, gi, 0, 0))
    v_spec = pl.BlockSpec((1, 1, VROWS, s), lambda bi, gi, i: (bi, gi, 0, 0))
    q_spec = pl.BlockSpec((1, 1, 1, d, cols), lambda bi, gi, i: (bi, gi, i, 0, 0))
    return pl.pallas_call(
        _sattn_kernel,
        grid=(b, g, nq),
        in_specs=[q_spec,
                  pl.BlockSpec((1, 1, 1, LANES, QT), lambda bi, gi, i: (bi, gi, i, 0, 0)),
                  k_spec(ksa.shape[3]), v_spec, k_spec(d), v_spec,
                  pl.BlockSpec((hg, N_NEAR, QT, QT), lambda bi, gi, i: (gi, 0, 0, 0)),
                  q_spec,
                  pl.BlockSpec((1, 1, 16, QT), lambda bi, gi, i: (bi, gi, 0, i))],
        out_specs=q_spec,
        out_shape=jax.ShapeDtypeStruct((b, g, nq, d, cols), BF16),
        scratch_shapes=[pltpu.VMEM((8, cols), F32),
                        pltpu.VMEM((VROWS, cols), F32),
                        pltpu.VMEM((8, cols), F32),
                        pltpu.VMEM((VROWS, cols), F32),
                        pltpu.VMEM((2, (N_NEAR // 2) * QT, cols), F32),
                        pltpu.VMEM((2, (N_NEAR // 2) * QT, cols), F32)],
        compiler_params=_cparams("parallel", "parallel", "arbitrary"),
        name="nsa_sel_win_attn",
    )(qt, selb, ksa, vst, kw, vwt, tt, oct, gates_t)


def _heads(t, h):
    b, s, _ = t.shape
    return t.reshape(b, s, h, HEAD_DIM).transpose(0, 2, 1, 3)


def _mixer_ab(x, g, scale, shift, w_in, conv_w, conv_b, dt_bias, a_log, d_skip, ssm_norm):
    b, s, d = x.shape
    o3 = 3 * SB_WIDTH
    qs = HEAD_DIM ** -0.5
    w_q, w_k, w_v = w_in[:, :SB_WIDTH] * qs, w_in[:, SB_WIDTH:2 * SB_WIDTH], w_in[:, 2 * SB_WIDTH:o3]
    w_z = w_in[:, o3:o3 + SSM_INNER]
    w_xbc = w_in[:, o3 + SSM_INNER:o3 + SSM_INNER + SSM_CONV_DIM]
    w_dt = w_in[:, o3 + SSM_INNER + SSM_CONV_DIM:]
    w_main = jnp.concatenate([w_xbc, w_z, w_q, w_k, w_v], axis=1).astype(BF16)
    w_dt = jnp.pad(w_dt, ((0, 0), (0, LANES - SSM_HEADS))).astype(BF16)
    proj, dt_raw = _norm_proj(x, g, scale, shift, w_main, w_dt, 512, "ab_in_proj")
    c0 = SSM_CONV_DIM + SSM_INNER
    o_sb = _sb_attention(proj, c0, c0 + SB_WIDTH, c0 + 2 * SB_WIDTH)
    y = _ssd(proj, dt_raw, conv_w, conv_b, dt_bias, a_log, d_skip, ssm_norm)
    return o_sb, y


def _mixer_nsa(x, g, scale, shift, tt, rel_bias, w_in, cmp_w1, cmp_b1, cmp_w2, cmp_pe):
    b, s, d = x.shape
    gq, hg = NSA_GROUPS, NSA_HG
    qs = HEAD_DIM ** -0.5
    n_main = NSA_Q + 6 * NSA_KV
    w_main = jnp.concatenate([w_in[:, :NSA_Q] * qs, w_in[:, NSA_Q:n_main]], axis=1).astype(BF16)
    n_gate = NSA_HEADS * N_BRANCH
    w_gate = jnp.pad(w_in[:, n_main:], ((0, 0), (0, LANES - n_gate))).astype(BF16)
    proj, gates = _norm_proj(x, g, scale, shift, w_main, w_gate, 512, "nsa_in_proj")
    gates_t = gates[..., :n_gate].reshape(b, s, gq, hg * N_BRANCH).transpose(0, 2, 3, 1)
    gates_t = jnp.pad(gates_t, ((0, 0), (0, 0), (0, 16 - hg * N_BRANCH), (0, 0)))
    nq = s // QT
    q5 = proj[..., :NSA_Q].reshape(b, nq, QT, gq, hg, HEAD_DIM)
    qt = q5.transpose(0, 3, 1, 5, 4, 2).reshape(b, gq, nq, HEAD_DIM, hg * QT)
    part = lambda j: proj[..., NSA_Q + j * NSA_KV:NSA_Q + (j + 1) * NSA_KV]
    nc = s // CMP_STRIDE
    kvc = jnp.stack([part(0), part(1)])
    a = kvc.reshape(2, b, nc, CMP_STRIDE, gq, HEAD_DIM).transpose(0, 1, 4, 2, 3, 5)
    a = a.reshape(2, b, gq, nc, CMP_STRIDE * HEAD_DIM)
    cmp = _compress(a, cmp_w1.astype(BF16), cmp_b1, cmp_w2.astype(BF16), cmp_pe)
    oct, selb = _cattn(rel_bias, qt, cmp[0], cmp[1].transpose(0, 1, 3, 2))
    grp = lambda t: t.reshape(b, s, gq, HEAD_DIM).transpose(0, 2, 1, 3)
    onehot = (jnp.arange(s)[:, None] // SEL_BLOCK == jnp.arange(LANES)[None, :]).astype(BF16)
    ksa = jnp.concatenate([grp(part(2)), jnp.broadcast_to(onehot, (b, gq, s, LANES))], axis=-1)

    def values_t(t):
        vt = t.reshape(b, s, gq, HEAD_DIM).transpose(0, 2, 3, 1)
        ones = jnp.ones((b, gq, 1, s), BF16)
        return jnp.concatenate([vt, ones, jnp.zeros((b, gq, VROWS - HEAD_DIM - 1, s), BF16)], axis=2)

    ot = _sattn(qt, selb, ksa, values_t(part(3)), grp(part(4)), values_t(part(5)), tt, oct, gates_t)
    o = ot.reshape(b, gq, nq, HEAD_DIM, hg, QT).transpose(0, 2, 5, 1, 4, 3)
    return o.reshape(b, s, NSA_Q)


def kernel(x, c, rel_bias, ada_w, ada_b, norm_g, ab_w_in, ab_conv_w, ab_conv_b, ab_dt_bias, ab_a_log,
           ab_d_skip, ab_ssm_norm, ab_w_out, nsa_w_in, nsa_cmp_w1, nsa_cmp_b1, nsa_cmp_w2, nsa_cmp_pe,
           nsa_w_out, ffn_w_up, ffn_conv_w, ffn_conv_b, ffn_w_down):
    depth = ada_w.shape[0]
    d = x.shape[-1]
    assert x.shape[1] % (CMP_STRIDE * QT) == 0 and x.shape[1] // SEL_BLOCK <= LANES
    mod = _modulation(c, ada_w, ada_b)
    tt = _bias_tiles(rel_bias) if depth > 1 else None
    for l in range(depth):
        shift1, scale1, gate1, shift2, scale2, gate2 = [mod[l, :, j * d:(j + 1) * d] for j in range(6)]
        if l % 2 == 0:
            e = l // 2
            o_sb, y = _mixer_ab(x, norm_g[l, 0], scale1, shift1, ab_w_in[e], ab_conv_w[e], ab_conv_b[e],
                                ab_dt_bias[e], ab_a_log[e], ab_d_skip[e], ab_ssm_norm[e])
            w_out = ab_w_out[e].astype(BF16)
            x = _out_proj([o_sb, y], [w_out[:SB_WIDTH], w_out[SB_WIDTH:]], norm_g[l, 1], gate1, x, "ab_out_proj")
        else:
            o = l // 2
            a = _mixer_nsa(x, norm_g[l, 0], scale1, shift1, tt, rel_bias, nsa_w_in[o], nsa_cmp_w1[o],
                           nsa_cmp_b1[o], nsa_cmp_w2[o], nsa_cmp_pe[o])
            x = _out_proj([a], [nsa_w_out[o].astype(BF16)], norm_g[l, 1], gate1, x, "nsa_out_proj")
        a = _ffn_up(x, norm_g[l, 2], scale2, shift2, ffn_w_up[l].astype(BF16), ffn_conv_w[l], ffn_conv_b[l])
        x = _out_proj([a], [ffn_w_down[l].astype(BF16)], norm_g[l, 3], gate2, x, "ffn_down_proj")
    return x
```

```python
import functools
import math

import numpy as np
import jax
import jax.numpy as jnp
from jax import lax
from jax.experimental import pallas as pl
from jax.experimental.pallas import tpu as pltpu

BF16 = jnp.bfloat16
F32 = jnp.float32

SB_HEADS = 8
HEAD_DIM = 64
SB_WIDTH = SB_HEADS * HEAD_DIM
SSM_HEADS = 16
SSM_INNER = SSM_HEADS * HEAD_DIM
SSM_GROUPS = 4
SSM_STATE = 128
SSM_CONV = 4
SSM_CHUNK = 128
SSM_CONV_DIM = SSM_INNER + 2 * SSM_GROUPS * SSM_STATE
NSA_HEADS = 16
NSA_GROUPS = 4
NSA_HG = NSA_HEADS // NSA_GROUPS
NSA_Q = NSA_HEADS * HEAD_DIM
NSA_KV = NSA_GROUPS * HEAD_DIM
CMP_BLOCK = 32
CMP_STRIDE = 16
SEL_BLOCK = 64
N_SEL = 16
WINDOW = 512
N_BRANCH = 3
REL_BUCKETS = 32
REL_MAX_EXACT = 16
REL_MAX_DIST = 1024
FFN_CONV = 3
EPS = 1e-6
NEG = -1e30
FORCE = 1e4
SEL_MASK = 2.0 ** 100
M_INIT = -3.0e38

QT = 128
LANES = 128
VMEM_LIMIT = 48 * 1024 * 1024


def _cparams(*sem):
    return pltpu.CompilerParams(dimension_semantics=sem, vmem_limit_bytes=VMEM_LIMIT)


def _sigmoid(x):
    return 1.0 / (1.0 + jnp.exp(-x))


def _silu(x):
    h = 0.5 * x
    return h + h * jnp.tanh(h)


def _dot(a, b):
    return jnp.dot(a, b, preferred_element_type=F32)


def _dot_nt(a, b):
    return lax.dot_general(a, b, (((1,), (1,)), ((), ())), preferred_element_type=F32)


def _rms(y, g):
    ms = jnp.mean(y * y, axis=-1, keepdims=True)
    return y * lax.rsqrt(ms + EPS) * g


def _mod_kernel(c_ref, w_ref, b_ref, o_ref):
    c = c_ref[...]
    s = c * _sigmoid(c)
    o_ref[0] = _dot(s.astype(BF16), w_ref[0].astype(BF16)) + b_ref[0]


def _modulation(c, ada_w, ada_b):
    depth, d, n = ada_w.shape
    b = c.shape[0]
    bp = 8
    tn = 768
    cp = jnp.pad(c, ((0, bp - b), (0, 0)))
    out = pl.pallas_call(
        _mod_kernel,
        grid=(depth, n // tn),
        in_specs=[pl.BlockSpec((bp, d), lambda l, j: (0, 0)),
                  pl.BlockSpec((1, d, tn), lambda l, j: (l, 0, j)),
                  pl.BlockSpec((1, 1, tn), lambda l, j: (l, 0, j))],
        out_specs=pl.BlockSpec((1, bp, tn), lambda l, j: (l, 0, j)),
        out_shape=jax.ShapeDtypeStruct((depth, bp, n), F32),
        compiler_params=_cparams("parallel", "parallel"),
        name="adaln_mod",
    )(cp, ada_w, ada_b.reshape(depth, 1, n))
    return out[:, :b]


def _norm_proj_kernel(x_ref, g_ref, sc_ref, sh_ref, w_ref, w2_ref, o_ref, o2_ref, *, tn):
    y = _rms(x_ref[0], g_ref[...])
    h = (y * (1.0 + sc_ref[0]) + sh_ref[0]).astype(BF16)
    for j in range(w_ref.shape[1] // tn):
        o_ref[0, :, j * tn:(j + 1) * tn] = _dot(h, w_ref[:, j * tn:(j + 1) * tn]).astype(o_ref.dtype)
    o2_ref[0] = _dot(h, w2_ref[...])


def _norm_proj(x, g, scale, shift, w, w_f32, tn, name):
    b, s, d = x.shape
    n = w.shape[1]
    tm = min(512, s)
    return pl.pallas_call(
        functools.partial(_norm_proj_kernel, tn=tn),
        grid=(b, s // tm),
        in_specs=[pl.BlockSpec((1, tm, d), lambda bi, i: (bi, i, 0)),
                  pl.BlockSpec((1, d), lambda bi, i: (0, 0)),
                  pl.BlockSpec((1, 1, d), lambda bi, i: (bi, 0, 0)),
                  pl.BlockSpec((1, 1, d), lambda bi, i: (bi, 0, 0)),
                  pl.BlockSpec((d, n), lambda bi, i: (0, 0)),
                  pl.BlockSpec((d, LANES), lambda bi, i: (0, 0))],
        out_specs=[pl.BlockSpec((1, tm, n), lambda bi, i: (bi, i, 0)),
                   pl.BlockSpec((1, tm, LANES), lambda bi, i: (bi, i, 0))],
        out_shape=[jax.ShapeDtypeStruct((b, s, n), BF16), jax.ShapeDtypeStruct((b, s, LANES), F32)],
        compiler_params=_cparams("parallel", "parallel"),
        name=name,
    )(x, g.reshape(1, d), scale.reshape(b, 1, d), shift.reshape(b, 1, d), w, w_f32)


def _out_proj_kernel(*refs, n_in):
    a_refs = refs[:n_in]
    w_refs = refs[n_in:2 * n_in]
    g_ref, gate_ref, x_ref, o_ref = refs[2 * n_in:]
    acc = _dot(a_refs[0][0], w_refs[0][...])
    for a_ref, w_ref in zip(a_refs[1:], w_refs[1:]):
        acc = acc + _dot(a_ref[0], w_ref[...])
    o_ref[0] = x_ref[0] + gate_ref[0] * _rms(acc, g_ref[...])


def _out_proj(acts, ws, g, gate, x, name):
    b, s, d = x.shape
    tm = min(512, s)
    n_in = len(acts)
    in_specs = [pl.BlockSpec((1, tm, a.shape[2]), lambda bi, i: (bi, i, 0)) for a in acts]
    in_specs += [pl.BlockSpec(w.shape, lambda bi, i: (0, 0)) for w in ws]
    in_specs += [pl.BlockSpec((1, d), lambda bi, i: (0, 0)),
                 pl.BlockSpec((1, 1, d), lambda bi, i: (bi, 0, 0)),
                 pl.BlockSpec((1, tm, d), lambda bi, i: (bi, i, 0))]
    return pl.pallas_call(
        functools.partial(_out_proj_kernel, n_in=n_in),
        grid=(b, s // tm),
        in_specs=in_specs,
        out_specs=pl.BlockSpec((1, tm, d), lambda bi, i: (bi, i, 0)),
        out_shape=jax.ShapeDtypeStruct((b, s, d), F32),
        compiler_params=_cparams("parallel", "parallel"),
        name=name,
    )(*acts, *ws, g.reshape(1, d), gate.reshape(b, 1, d), x)


def _shift_rows(cur, prev, k):
    sc = pltpu.roll(cur, k, 0)
    sp = pltpu.roll(prev, k, 0)
    row8 = lax.broadcasted_iota(jnp.int32, sp.shape, 0)
    head = jnp.where(row8 < k, sp, sc[:8])
    return jnp.concatenate([head, sc[8:]], axis=0)


def _gelu_tanh_x2(x):
    c = math.sqrt(2.0 / math.pi)
    return x * (1.0 + jnp.tanh(x * (c + (c * 0.044715) * (x * x))))


def _ffn_up_kernel(x_ref, xp_ref, g_ref, sc_ref, sh_ref, w_ref, cw_ref, cb_ref, o_ref, u_ref, *, dff, ck):
    def modulated(xv):
        return (_rms(xv, g_ref[...]) * (1.0 + sc_ref[0]) + sh_ref[0]).astype(BF16)

    tm = x_ref.shape[1]
    h = modulated(x_ref[0])
    hp = modulated(xp_ref[0])
    first = pl.program_id(1) == 0

    def conv(slot, col0):
        w_up = w_ref[:, col0:col0 + ck]
        cur = _dot(h, w_up)
        u_ref[slot, 0:8, :] = jnp.where(first, 0.0, _dot(hp, w_up))
        u_ref[slot, 8:8 + tm, :] = cur
        w = cw_ref[:, col0:col0 + ck]
        y = w[2:3] * cur + cb_ref[:, col0:col0 + ck]
        y = y + w[1:2] * u_ref[slot, 7:7 + tm, :]
        return y + w[0:1] * u_ref[slot, 6:6 + tm, :]

    for c in range(dff // ck):
        act = _gelu_tanh_x2(conv(0, c * ck)) * conv(1, dff + c * ck)
        o_ref[0, :, c * ck:(c + 1) * ck] = act.astype(o_ref.dtype)


def _ffn_up(x, g, scale, shift, w_up, conv_w, conv_b):
    b, s, d = x.shape
    c2 = w_up.shape[1]
    dff = c2 // 2
    tm = min(512, s)
    half = jnp.concatenate([jnp.ones((dff,), F32), jnp.full((dff,), 0.5, F32)])
    cw = jnp.pad(conv_w * half, ((0, 8 - conv_w.shape[0]), (0, 0)))
    conv_b = conv_b * half
    return pl.pallas_call(
        functools.partial(_ffn_up_kernel, dff=dff, ck=256),
        grid=(b, s // tm),
        in_specs=[pl.BlockSpec((1, tm, d), lambda bi, i: (bi, i, 0)),
                  pl.BlockSpec((1, 8, d), lambda bi, i: (bi, jnp.maximum(i * (tm // 8) - 1, 0), 0)),
                  pl.BlockSpec((1, d), lambda bi, i: (0, 0)),
                  pl.BlockSpec((1, 1, d), lambda bi, i: (bi, 0, 0)),
                  pl.BlockSpec((1, 1, d), lambda bi, i: (bi, 0, 0)),
                  pl.BlockSpec((d, c2), lambda bi, i: (0, 0)),
                  pl.BlockSpec((8, c2), lambda bi, i: (0, 0)),
                  pl.BlockSpec((1, c2), lambda bi, i: (0, 0))],
        out_specs=pl.BlockSpec((1, tm, dff), lambda bi, i: (bi, i, 0)),
        out_shape=jax.ShapeDtypeStruct((b, s, dff), BF16),
        scratch_shapes=[pltpu.VMEM((2, 8 + tm, 256), F32)],
        compiler_params=_cparams("parallel", "parallel"),
        name="ffn_up_conv_act",
    )(x, x, g.reshape(1, d), scale.reshape(b, 1, d), shift.reshape(b, 1, d), w_up, cw, conv_b.reshape(1, c2))


SBQ = 256


SB_TILES = 2
SB_DEAD = -104.0


def _sb_kernel(q_ref, k_ref, v_ref, u_ref, o_ref, acc_ref, cs_ref):
    i = pl.program_id(2)
    qp = q_ref[0]
    lane_lo = lax.broadcasted_iota(jnp.int32, (SBQ, LANES), 1) < HEAD_DIM
    zero = jnp.zeros_like(qp)
    qs = [jnp.where(lane_lo, qp, zero), jnp.where(lane_lo, zero, qp)]
    nsub = SBQ // QT
    diff = (lax.broadcasted_iota(jnp.int32, (SBQ, QT), 1) - lax.broadcasted_iota(jnp.int32, (SBQ, QT), 0))
    acc_ref[...] = jnp.zeros(acc_ref.shape, F32)
    cs_ref[...] = jnp.zeros(cs_ref.shape, F32)

    def tiles(kidxs, diag):
        offs = [pl.multiple_of(kidx * QT, QT) for kidx in kidxs]
        ks = [k_ref[0, pl.ds(k0, QT), :] for k0 in offs]
        zs = [[_dot_nt(q, k) for k in ks] for q in qs]
        css = [[], []]
        belows = [diff < i * SBQ - kidx * QT for kidx in kidxs] if diag else None
        for h in range(2):
            for t, z in enumerate(zs[h]):
                nl = jnp.maximum(z, 0.0) + jnp.log(1.0 + jnp.exp(-jnp.abs(z)))
                if diag:
                    nl = jnp.where(belows[t], nl, 0.0)
                hi = nl.astype(BF16)
                lo = (nl - hi.astype(F32)).astype(BF16)
                css[h].append(_dot(jnp.concatenate([hi, lo], axis=1), u_ref[...]))
        vs = [v_ref[0, pl.ds(k0, QT), :] for k0 in offs]
        for h in range(2):
            csum = cs_ref[h]
            acc = acc_ref[h]
            for t, (z, cs) in enumerate(zip(zs[h], css[h])):
                w = jnp.exp(z + cs[:, :QT] + csum)
                if diag:
                    w = jnp.where(belows[t], w, 0.0)
                acc = acc + _dot(w.astype(BF16), vs[t])
                csum = csum + cs[:, QT:]
            acc_ref[h] = acc
            cs_ref[h] = csum

    for r0 in range(nsub - 1, -1, -SB_TILES):
        tiles([i * nsub + r0 - r for r in range(SB_TILES)], True)

    n_trips = i * (nsub // SB_TILES)

    def cond(carry):
        it, cmax = carry
        return jnp.logical_and(it < n_trips, cmax > SB_DEAD)

    def body(carry):
        it, _ = carry
        tiles([i * nsub - 1 - it * SB_TILES - r for r in range(SB_TILES)], False)
        return it + 1, jnp.max(cs_ref[...])

    lax.while_loop(cond, body, (jnp.int32(0), jnp.max(cs_ref[...])))
    o_ref[0] = jnp.where(lane_lo, acc_ref[0], acc_ref[1]).astype(o_ref.dtype)


def _suffix_matrix():
    j = np.arange(2 * QT)[:, None] % QT
    s = np.arange(QT + LANES)[None, :]
    return jnp.asarray(-np.where(s < QT, j >= s, True).astype(np.float32), dtype=BF16)


def _sb_attention(proj, q_col, k_col, v_col):
    b, s, _ = proj.shape
    pairs = SB_HEADS // 2
    qb, kb, vb = q_col // LANES, k_col // LANES, v_col // LANES
    return pl.pallas_call(
        _sb_kernel,
        grid=(b, pairs, s // SBQ),
        in_specs=[pl.BlockSpec((1, SBQ, LANES), lambda bi, p, i: (bi, i, qb + p)),
                  pl.BlockSpec((1, s, LANES), lambda bi, p, i: (bi, 0, kb + p)),
                  pl.BlockSpec((1, s, LANES), lambda bi, p, i: (bi, 0, vb + p)),
                  pl.BlockSpec((2 * QT, QT + LANES), lambda bi, p, i: (0, 0))],
        out_specs=pl.BlockSpec((1, SBQ, LANES), lambda bi, p, i: (bi, i, p)),
        out_shape=jax.ShapeDtypeStruct((b, s, SB_WIDTH), BF16),
        scratch_shapes=[pltpu.VMEM((2, SBQ, LANES), F32), pltpu.VMEM((2, SBQ, LANES), F32)],
        compiler_params=_cparams("parallel", "parallel", "arbitrary"),
        name="sb_attention",
    )(proj, proj, proj, _suffix_matrix())


def _pair(f, p, lane_lo):
    return jnp.where(lane_lo, f[:, 2 * p:2 * p + 1], f[:, 2 * p + 1:2 * p + 2])


def _ssd_kernel(xr_ref, xp_ref, dt_ref, z_ref, cw_ref, cb_ref, dtb_ref, alog_ref, dsk_ref, nrm_ref,
                o_ref, st_ref, xc_ref, y_ref):
    ln = SSM_CHUNK
    c = pl.program_id(1)

    @pl.when(c == 0)
    def _():
        st_ref[...] = jnp.zeros(st_ref.shape, F32)

    first = c == 0
    ck = 256
    for j in range(SSM_CONV_DIM // ck):
        cur = xr_ref[0, :, j * ck:(j + 1) * ck].astype(F32)
        prev = xp_ref[0, :, j * ck:(j + 1) * ck].astype(F32)
        prev = jnp.where(first, 0.0, prev)
        w = cw_ref[:, j * ck:(j + 1) * ck]
        y = w[3:4] * cur + cb_ref[:, j * ck:(j + 1) * ck]
        for kk in range(1, SSM_CONV):
            y = y + w[3 - kk:4 - kk] * _shift_rows(cur, prev, kk)
        xc_ref[:, j * ck:(j + 1) * ck] = _silu(y)

    dtr = dt_ref[0] + dtb_ref[...]
    dt = jnp.maximum(dtr, 0.0) + jnp.log(1.0 + jnp.exp(-jnp.abs(dtr)))
    a = dt * (-jnp.exp(alog_ref[...]))
    row = lax.broadcasted_iota(jnp.int32, (ln, ln), 0)
    col = lax.broadcasted_iota(jnp.int32, (ln, ln), 1)
    causal = col <= row
    acs = jnp.dot(causal.astype(F32), a, precision=lax.Precision.HIGHEST, preferred_element_type=F32)
    acs_t = acs.T
    eacs = jnp.exp(acs)
    dout = jnp.exp(acs[ln - 1:ln, :] - acs)
    lane_lo = lax.broadcasted_iota(jnp.int32, (ln, LANES), 1) < HEAD_DIM

    hg = SSM_HEADS // SSM_GROUPS
    for g in range(SSM_GROUPS):
        bm = xc_ref[:, SSM_INNER + g * SSM_STATE:SSM_INNER + (g + 1) * SSM_STATE]
        cm = xc_ref[:, SSM_INNER + (SSM_GROUPS + g) * SSM_STATE:SSM_INNER + (SSM_GROUPS + g + 1) * SSM_STATE]
        cmb = cm.astype(BF16)
        cb = _dot_nt(cmb, bm.astype(BF16))
        prev = st_ref[g]
        yoff = _dot(cmb, prev.astype(BF16))
        dox_parts = []
        cdec_parts = []
        for pp in range(hg // 2):
            p = g * (hg // 2) + pp
            xs = xc_ref[:, p * LANES:(p + 1) * LANES]
            xdt = xs * _pair(dt, p, lane_lo)
            ms = []
            for hh in (2 * p, 2 * p + 1):
                seg = acs[:, hh:hh + 1] - acs_t[hh:hh + 1, :]
                decay = jnp.exp(jnp.where(causal, seg, NEG))
                ms.append((cb * decay).astype(BF16))
            rhs = jnp.concatenate([jnp.where(lane_lo, xdt, 0.0), jnp.where(lane_lo, 0.0, xdt)], axis=0)
            ydiag = _dot(jnp.concatenate(ms, axis=1), rhs.astype(BF16))
            yo = yoff[:, pp * LANES:(pp + 1) * LANES] * _pair(eacs, p, lane_lo)
            y_ref[:, p * LANES:(p + 1) * LANES] = ydiag + yo + xs * dsk_ref[:, p * LANES:(p + 1) * LANES]
            dox_parts.append((xdt * _pair(dout, p, lane_lo)).astype(BF16))
            cdec_parts.append(_pair(eacs[ln - 1:ln, :], p, lane_lo[0:1]))
        states = _dot(bm.T.astype(BF16), jnp.concatenate(dox_parts, axis=1))
        st_ref[g] = prev * jnp.concatenate(cdec_parts, axis=1) + states

    zf = z_ref[0].astype(F32)
    yz = y_ref[...] * _silu(zf)
    o_ref[0] = _rms(yz, nrm_ref[...]).astype(o_ref.dtype)


def _ssd(proj, dt_raw, conv_w, conv_b, dt_bias, a_log, d_skip, ssm_norm):
    b, s, _ = proj.shape
    ln = SSM_CHUNK
    cw = jnp.pad(conv_w, ((0, 8 - SSM_CONV), (0, 0)))
    pad = LANES - SSM_HEADS
    return pl.pallas_call(
        _ssd_kernel,
        grid=(b, s // ln),
        in_specs=[pl.BlockSpec((1, ln, SSM_CONV_DIM), lambda bi, c: (bi, c, 0)),
                  pl.BlockSpec((1, 8, SSM_CONV_DIM), lambda bi, c: (bi, jnp.maximum(c * (ln // 8) - 1, 0), 0)),
                  pl.BlockSpec((1, ln, LANES), lambda bi, c: (bi, c, 0)),
                  pl.BlockSpec((1, ln, SSM_INNER), lambda bi, c: (bi, c, SSM_CONV_DIM // SSM_INNER)),
                  pl.BlockSpec((8, SSM_CONV_DIM), lambda bi, c: (0, 0)),
                  pl.BlockSpec((1, SSM_CONV_DIM), lambda bi, c: (0, 0)),
                  pl.BlockSpec((1, LANES), lambda bi, c: (0, 0)),
                  pl.BlockSpec((1, LANES), lambda bi, c: (0, 0)),
                  pl.BlockSpec((1, SSM_INNER), lambda bi, c: (0, 0)),
                  pl.BlockSpec((1, SSM_INNER), lambda bi, c: (0, 0))],
        out_specs=pl.BlockSpec((1, ln, SSM_INNER), lambda bi, c: (bi, c, 0)),
        out_shape=jax.ShapeDtypeStruct((b, s, SSM_INNER), BF16),
        scratch_shapes=[pltpu.VMEM((SSM_GROUPS, SSM_STATE, SSM_INNER // SSM_GROUPS), F32),
                        pltpu.VMEM((ln, SSM_CONV_DIM), F32),
                        pltpu.VMEM((ln, SSM_INNER), F32)],
        compiler_params=_cparams("parallel", "arbitrary"),
        name="ssd",
    )(proj, proj, dt_raw, proj, cw, conv_b.reshape(1, -1),
      jnp.pad(dt_bias, (0, pad)).reshape(1, LANES), jnp.pad(a_log, (0, pad)).reshape(1, LANES),
      jnp.repeat(d_skip, HEAD_DIM).reshape(1, SSM_INNER), ssm_norm.reshape(1, SSM_INNER))


def _rel_bucket(n):
    n = jnp.maximum(n, 0)
    nf = jnp.maximum(n, 1).astype(F32)
    large = REL_MAX_EXACT + (jnp.log(nf / REL_MAX_EXACT) / math.log(REL_MAX_DIST / REL_MAX_EXACT)
                             * (REL_BUCKETS - REL_MAX_EXACT)).astype(jnp.int32)
    large = jnp.minimum(large, REL_BUCKETS - 1)
    return jnp.where(n < REL_MAX_EXACT, n, large)


def _bias_lookup(rb_ref, head, bucket):
    out = jnp.full(bucket.shape, rb_ref[head, 0], F32)
    for kk in range(1, REL_BUCKETS):
        out = jnp.where(bucket >= kk, rb_ref[head, kk], out)
    return out


N_NEAR = 8


def _bias_tiles_kernel(rb_ref, o_ref):
    h = pl.program_id(0)
    row = lax.broadcasted_iota(jnp.int32, (QT, QT), 0)
    col = lax.broadcasted_iota(jnp.int32, (QT, QT), 1)
    for d in range(N_NEAR):
        bucket = _rel_bucket(d * QT + col - row)
        o_ref[0, d] = _bias_lookup(rb_ref, h, bucket) - rb_ref[h, REL_BUCKETS - 1]


def _bias_tiles(rel_bias):
    return pl.pallas_call(
        _bias_tiles_kernel,
        grid=(NSA_HEADS,),
        in_specs=[pl.BlockSpec(memory_space=pltpu.SMEM)],
        out_specs=pl.BlockSpec((1, N_NEAR, QT, QT), lambda h: (h, 0, 0, 0)),
        out_shape=jax.ShapeDtypeStruct((NSA_HEADS, N_NEAR, QT, QT), F32),
        compiler_params=_cparams("parallel"),
        name="nsa_bias_tiles",
    )(rel_bias)


def _compress_kernel(a_ref, w1_ref, b1_ref, w2_ref, pe_ref, o_ref):
    a = a_ref[0, 0, 0]
    ncp = a.shape[0]
    half = CMP_STRIDE * HEAD_DIM
    h1 = _dot(a, w1_ref[0, :half, :])
    h2 = _dot(a, w1_ref[0, half:, :])
    h2 = pltpu.roll(h2, ncp - 1, 0)
    rowi = lax.broadcasted_iota(jnp.int32, h2.shape, 0)
    h2 = jnp.where(rowi < ncp - 1, h2, 0.0)
    pev = _dot(pe_ref[0], w1_ref[0])[0:1]
    hid = h1 + h2 + pev + b1_ref[0]
    act = hid * _sigmoid(hid)
    o_ref[0, 0, 0] = _dot(act.astype(BF16), w2_ref[0]).astype(o_ref.dtype)


def _compress(a, w1, b1, w2, pe):
    _, b, g, ncp, kk = a.shape
    hid = w1.shape[2]
    pe8 = jnp.pad(pe.reshape(2, 1, CMP_BLOCK * HEAD_DIM), ((0, 0), (0, 7), (0, 0))).astype(BF16)
    return pl.pallas_call(
        _compress_kernel,
        grid=(2, b, g),
        in_specs=[pl.BlockSpec((1, 1, 1, ncp, kk), lambda t, bi, gi: (t, bi, gi, 0, 0)),
                  pl.BlockSpec((1, 2 * kk, hid), lambda t, bi, gi: (t, 0, 0)),
                  pl.BlockSpec((1, 1, hid), lambda t, bi, gi: (t, 0, 0)),
                  pl.BlockSpec((1, hid, HEAD_DIM), lambda t, bi, gi: (t, 0, 0)),
                  pl.BlockSpec((1, 8, 2 * kk), lambda t, bi, gi: (t, 0, 0))],
        out_specs=pl.BlockSpec((1, 1, 1, ncp, HEAD_DIM), lambda t, bi, gi: (t, bi, gi, 0, 0)),
        out_shape=jax.ShapeDtypeStruct((2, b, g, ncp, HEAD_DIM), BF16),
        compiler_params=_cparams("parallel", "parallel", "parallel"),
        name="nsa_compress",
    )(a, w1, b1.reshape(2, 1, hid), w2, pe8)


BUCKET_CONST_FROM = 800
CMP_SPAN = CMP_STRIDE * QT
CMP_TILE_REACH = CMP_STRIDE * (QT - 1) + CMP_BLOCK - 1
N_CMP_TILES = -(-(BUCKET_CONST_FROM + CMP_TILE_REACH) // QT)


def _cmp_bias_tiles_kernel(rb_ref, o_ref):
    h = pl.program_id(0)
    t0 = lax.broadcasted_iota(jnp.int32, (QT, QT), 1)
    cend = lax.broadcasted_iota(jnp.int32, (QT, QT), 0) * CMP_STRIDE + (CMP_BLOCK - 1)
    for k in range(N_CMP_TILES):
        t = t0 + k * QT
        o_ref[0, k] = jnp.where(cend <= t, _bias_lookup(rb_ref, h, _rel_bucket(t - cend)), NEG)


def _cmp_bias_tiles(rel_bias):
    return pl.pallas_call(
        _cmp_bias_tiles_kernel,
        grid=(NSA_HEADS,),
        in_specs=[pl.BlockSpec(memory_space=pltpu.SMEM)],
        out_specs=pl.BlockSpec((1, N_CMP_TILES, QT, QT), lambda h: (h, 0, 0, 0)),
        out_shape=jax.ShapeDtypeStruct((NSA_HEADS, N_CMP_TILES, QT, QT), F32),
        compiler_params=_cparams("parallel"),
        name="nsa_cmp_bias_tiles",
    )(rel_bias)


def _cattn_kernel(rb_ref, qt_ref, k_ref, vt_ref, ov_ref, tab_ref, oc_ref, sel_ref, bias_ref):
    g = pl.program_id(0)
    i = pl.program_id(1)
    nb = k_ref.shape[0]
    ncp = k_ref.shape[2]
    q0 = i * QT
    for r in range(ncp // QT):
        rows = slice(r * QT, (r + 1) * QT)
        min_rel = q0 - (CMP_STRIDE * (r * QT + QT - 1) + CMP_BLOCK - 1)
        max_rel = q0 + QT - 1 - (CMP_STRIDE * r * QT + CMP_BLOCK - 1)

        @pl.when((min_rel < BUCKET_CONST_FROM) & (max_rel >= 0))
        def _(r=r, rows=rows):
            k = i - r * (CMP_SPAN // QT)
            for h in range(NSA_HG):
                bias_ref[h, rows, :] = tab_ref[h, k]

        @pl.when(min_rel >= BUCKET_CONST_FROM)
        def _(rows=rows):
            for h in range(NSA_HG):
                bias_ref[h, rows, :] = jnp.full((QT, QT), rb_ref[g * NSA_HG + h, REL_BUCKETS - 1], F32)

        @pl.when(max_rel < 0)
        def _(rows=rows):
            for h in range(NSA_HG):
                bias_ref[h, rows, :] = jnp.full((QT, QT), NEG, F32)

    anyvalid = q0 + lax.broadcasted_iota(jnp.int32, (1, QT), 1) >= CMP_BLOCK - 1
    qks = [_dot(k_ref[bi, 0], qt_ref[bi, 0, 0]) for bi in range(nb)]
    pcsums = []
    for bi in range(nb):
        vt = vt_ref[bi, 0]
        pcsum = jnp.zeros((ncp, QT), F32)
        ocs = []
        for h in range(NSA_HG):
            lg = qks[bi][:, h * QT:(h + 1) * QT] + bias_ref[h]
            m = jnp.max(lg, axis=0, keepdims=True)
            e = jnp.exp(lg - m)
            p = e * jnp.where(anyvalid, 1.0 / jnp.sum(e, axis=0, keepdims=True), 0.0)
            pcsum = pcsum + p
            ocs.append(_dot(vt, p.astype(BF16)))
        oc_ref[bi, 0, 0] = jnp.concatenate(ocs, axis=1).astype(oc_ref.dtype)
        pcsums.append(pcsum)

    imps = []
    for pcsum in pcsums:
        hi = pcsum.astype(BF16)
        lo = (pcsum - hi.astype(F32)).astype(BF16)
        imps.append(_dot(ov_ref[...], hi) + _dot(ov_ref[...], lo))
    jrow = lax.broadcasted_iota(jnp.int32, (LANES, QT), 0)
    tq = q0 + lax.broadcasted_iota(jnp.int32, (LANES, QT), 1)
    cur = jnp.right_shift(tq, SEL_BLOCK.bit_length() - 1)
    forced = (jrow == 0) | (jrow == cur) | (jrow == cur - 1)
    jf = jrow.astype(F32)
    for bi, imp in enumerate(imps):
        val = jnp.where(forced, FORCE, jnp.where(jrow * SEL_BLOCK <= tq, imp, -1.0))
        sel = jnp.zeros((LANES, QT), F32)
        for _ in range(N_SEL):
            m = jnp.max(val, axis=0, keepdims=True)
            jm = jnp.min(jnp.where(val == m, jf, 1e9), axis=0, keepdims=True)
            hit = jf == jm
            sel = jnp.where(hit, 1.0, sel)
            val = jnp.where(hit, M_INIT, val)
        sel_ref[bi, 0, 0] = ((sel - 1.0) * SEL_MASK).astype(sel_ref.dtype)


def _overlap_t(ncp):
    ci = (np.arange(ncp) * CMP_STRIDE)[None, :]
    sj = (np.arange(LANES) * SEL_BLOCK)[:, None]
    return jnp.asarray(((ci < sj + SEL_BLOCK) & (ci + CMP_BLOCK > sj)).astype(np.float32), dtype=BF16)


def _cattn(rel_bias, qt, kcmp, vcmp_t, cmp_tiles):
    b, g, nq, d, cols = qt.shape
    ncp = kcmp.shape[2]
    hg = NSA_HG
    q_spec = pl.BlockSpec((b, 1, 1, d, cols), lambda gi, i: (0, gi, i, 0, 0))
    return pl.pallas_call(
        _cattn_kernel,
        grid=(g, nq),
        in_specs=[pl.BlockSpec(memory_space=pltpu.SMEM),
                  q_spec,
                  pl.BlockSpec((b, 1, ncp, d), lambda gi, i: (0, gi, 0, 0)),
                  pl.BlockSpec((b, 1, d, ncp), lambda gi, i: (0, gi, 0, 0)),
                  pl.BlockSpec((LANES, ncp), lambda gi, i: (0, 0)),
                  pl.BlockSpec((hg, N_CMP_TILES, QT, QT), lambda gi, i: (gi, 0, 0, 0))],
        out_specs=[q_spec,
                   pl.BlockSpec((b, 1, 1, LANES, QT), lambda gi, i: (0, gi, i, 0, 0))],
        out_shape=[jax.ShapeDtypeStruct((b, g, nq, d, cols), BF16),
                   jax.ShapeDtypeStruct((b, g, nq, LANES, QT), BF16)],
        scratch_shapes=[pltpu.VMEM((hg, ncp, QT), F32)],
        compiler_params=_cparams("parallel", "parallel"),
        name="nsa_cmp_attn",
    )(rel_bias, qt, kcmp, vcmp_t, _overlap_t(ncp), cmp_tiles)


VROWS = 80


def _sattn_kernel(qt_ref, sel_ref, ks_ref, vs_ref, kw_ref, vw_ref, tt_ref, oc_ref, gt_ref, o_ref,
                  m_ref, acc_ref, mw_ref, accw_ref, s0_ref, s1_ref):
    i = pl.program_id(2)
    cols = NSA_HG * QT
    qt = qt_ref[0, 0, 0]
    qa = jnp.concatenate([qt, jnp.concatenate([sel_ref[0, 0, 0]] * NSA_HG, axis=1)], axis=0)
    krow = lax.broadcasted_iota(jnp.int32, (QT, cols), 0)
    qcol = lax.broadcasted_iota(jnp.int32, (QT, cols), 1) & (QT - 1)
    sel_state = (m_ref, acc_ref)
    win_state = (mw_ref, accw_ref)

    for mr, ar in (sel_state, win_state):
        mr[...] = jnp.full(mr.shape, M_INIT, F32)
        ar[...] = jnp.zeros(ar.shape, F32)

    def update_state(state, *tiles):
        mr, ar = state
        m = mr[...]
        acc = ar[...]
        for s, vt in tiles:
            m_new = jnp.maximum(m, jnp.max(s, axis=0, keepdims=True))
            p = jnp.exp(s - m_new[0:1])
            acc = acc * jnp.exp(m - m_new)[0:1] + _dot(vt, p.astype(BF16))
            m = m_new
        ar[...] = acc
        mr[...] = m

    def update(*tiles):
        update_state(sel_state, *tiles)

    def near_bias(d):
        return jnp.concatenate([tt_ref[h, d] for h in range(NSA_HG)], axis=1)

    def finish(state):
        acc = state[1][...]
        return acc[:HEAD_DIM] / acc[HEAD_DIM:HEAD_DIM + 1]

    def sel_tile(kt, d, size=QT):
        k0 = pl.multiple_of(kt * QT, QT)
        s = _dot(ks_ref[0, 0, pl.ds(k0, size), :], qa)
        if d is not None:
            s = s + near_bias(d)
        return s, vs_ref[0, 0, :, pl.ds(k0, size)]

    def group(k_ref, v_ref, q_op, kt0, ds, first_gt, last_le):
        n = len(ds)
        k0 = pl.multiple_of(kt0 * QT, QT)
        s = _dot(k_ref[0, 0, pl.ds(k0, n * QT), :], q_op)
        parts = []
        for t, d in enumerate(ds):
            blk = s[t * QT:(t + 1) * QT] + near_bias(d)
            if t == 0 and first_gt:
                blk = jnp.where(krow > qcol, blk, NEG)
            if t == n - 1 and last_le:
                blk = jnp.where(krow <= qcol, blk, NEG)
            parts.append(blk)
        return jnp.concatenate(parts, axis=0), v_ref[0, 0, :, pl.ds(k0, n * QT)]

    big = (N_NEAR // 2) * QT
    nwin = WINDOW // QT

    def window_tile():
        return group(kw_ref, vw_ref, qt, i - nwin, list(range(nwin, -1, -1)), True, True)

    @pl.when(i >= N_NEAR - 1)
    def _():
        n_far = i - (N_NEAR - 1)
        n_chunks = (n_far + N_NEAR - 1) // N_NEAR

        def start(c):
            return jnp.maximum(n_far - N_NEAR * (n_chunks - c), 0)

        def logits(c, dst_ref):
            for hf in range(2):
                k0 = pl.multiple_of(start(c) * QT + hf * big, QT)
                dst_ref[hf] = _dot(ks_ref[0, 0, pl.ds(k0, big), :], qa)

        def values(c, hf):
            k0 = pl.multiple_of(start(c) * QT + hf * big, QT)
            return vs_ref[0, 0, :, pl.ds(k0, big)]

        key_lane = lax.broadcasted_iota(jnp.int32, (VROWS, big), 1)

        def far_step(c, src_ref, dst_ref):
            logits(c + 1, dst_ref)
            own = (start(c + 1) - start(c)) * QT
            tiles = []
            for hf in range(2):
                vt = values(c, hf)
                tiles.append((src_ref[hf], jnp.where(key_lane < own - hf * big, vt, jnp.zeros_like(vt))))
            update(*tiles)

        def near_step(src_ref):
            win = window_tile()
            tiles = []
            for hf in range(2):
                parts = []
                for t in range(N_NEAR // 2):
                    d = N_NEAR - 1 - (hf * (N_NEAR // 2) + t)
                    blk = src_ref[hf, t * QT:(t + 1) * QT, :] + near_bias(d)
                    if d == 0:
                        blk = jnp.where(krow <= qcol, blk, NEG)
                    parts.append(blk)
                tiles.append((jnp.concatenate(parts, axis=0), values(n_chunks, hf)))
            update(*tiles)
            update_state(win_state, win)

        logits(0, s0_ref)

        def far_body(j, carry):
            far_step(2 * j, s0_ref, s1_ref)
            far_step(2 * j + 1, s1_ref, s0_ref)
            return carry

        lax.fori_loop(0, n_chunks // 2, far_body, 0)

        @pl.when(n_chunks % 2 == 1)
        def _():
            far_step(n_chunks - 1, s0_ref, s1_ref)
            near_step(s1_ref)

        @pl.when(n_chunks % 2 == 0)
        def _():
            near_step(s0_ref)

    def near_body(kt, carry):
        update(sel_tile(kt, i - kt))
        return carry

    @pl.when(i < N_NEAR - 1)
    def _():
        lax.fori_loop(0, i, near_body, 0)
        s, vt = sel_tile(i, 0)
        update((jnp.where(krow <= qcol, s, NEG), vt))

        @pl.when(i >= nwin)
        def _():
            update_state(win_state, window_tile())

        @pl.when(i < nwin)
        def _():
            for d in range(nwin - 1, -1, -1):
                @pl.when(i - d >= 0)
                def _(d=d):
                    update_state(win_state, group(kw_ref, vw_ref, qt, i - d, [d], False, d == 0))

    osl = finish(sel_state)
    ow = finish(win_state)

    gt = _sigmoid(gt_ref[0, 0])
    gate = lambda br: jnp.concatenate([gt[3 * h + br:3 * h + br + 1] for h in range(NSA_HG)], axis=1)
    o = gate(0) * oc_ref[0, 0, 0].astype(F32) + gate(1) * osl + gate(2) * ow
    o_ref[0, 0, 0] = o.astype(o_ref.dtype)


def _sattn(qt, selb, ksa, vst, kw, vwt, tt, oct, gates_t):
    b, g, nq, d, cols = qt.shape
    s = kw.shape[2]
    hg = NSA_HG
    k_spec = lambda width: pl.BlockSpec((1, 1, s, width), lambda bi, gi, i: (bi, gi, 0, 0))
    v_spec = pl.BlockSpec((1, 1, VROWS, s), lambda bi, gi, i: (bi, gi, 0, 0))
    q_spec = pl.BlockSpec((1, 1, 1, d, cols), lambda bi, gi, i: (bi, gi, i, 0, 0))
    return pl.pallas_call(
        _sattn_kernel,
        grid=(b, g, nq),
        in_specs=[q_spec,
                  pl.BlockSpec((1, 1, 1, LANES, QT), lambda bi, gi, i: (bi, gi, i, 0, 0)),
                  k_spec(ksa.shape[3]), v_spec, k_spec(d), v_spec,
                  pl.BlockSpec((hg, N_NEAR, QT, QT), lambda bi, gi, i: (gi, 0, 0, 0)),
                  q_spec,
                  pl.BlockSpec((1, 1, 16, QT), lambda bi, gi, i: (bi, gi, 0, i))],
        out_specs=q_spec,
        out_shape=jax.ShapeDtypeStruct((b, g, nq, d, cols), BF16),
        scratch_shapes=[pltpu.VMEM((8, cols), F32),
                        pltpu.VMEM((VROWS, cols), F32),
                        pltpu.VMEM((8, cols), F32),
                        pltpu.VMEM((VROWS, cols), F32),
                        pltpu.VMEM((2, (N_NEAR // 2) * QT, cols), F32),
                        pltpu.VMEM((2, (N_NEAR // 2) * QT, cols), F32)],
        compiler_params=_cparams("parallel", "parallel", "arbitrary"),
        name="nsa_sel_win_attn",
    )(qt, selb, ksa, vst, kw, vwt, tt, oct, gates_t)


def _heads(t, h):
    b, s, _ = t.shape
    return t.reshape(b, s, h, HEAD_DIM).transpose(0, 2, 1, 3)


def _mixer_ab(x, g, scale, shift, w_in, conv_w, conv_b, dt_bias, a_log, d_skip, ssm_norm):
    b, s, d = x.shape
    o3 = 3 * SB_WIDTH
    qs = HEAD_DIM ** -0.5
    w_q, w_k, w_v = w_in[:, :SB_WIDTH] * qs, w_in[:, SB_WIDTH:2 * SB_WIDTH], w_in[:, 2 * SB_WIDTH:o3]
    w_z = w_in[:, o3:o3 + SSM_INNER]
    w_xbc = w_in[:, o3 + SSM_INNER:o3 + SSM_INNER + SSM_CONV_DIM]
    w_dt = w_in[:, o3 + SSM_INNER + SSM_CONV_DIM:]
    w_main = jnp.concatenate([w_xbc, w_z, w_q, w_k, w_v], axis=1).astype(BF16)
    w_dt = jnp.pad(w_dt, ((0, 0), (0, LANES - SSM_HEADS))).astype(BF16)
    proj, dt_raw = _norm_proj(x, g, scale, shift, w_main, w_dt, 512, "ab_in_proj")
    c0 = SSM_CONV_DIM + SSM_INNER
    o_sb = _sb_attention(proj, c0, c0 + SB_WIDTH, c0 + 2 * SB_WIDTH)
    y = _ssd(proj, dt_raw, conv_w, conv_b, dt_bias, a_log, d_skip, ssm_norm)
    return o_sb, y


def _mixer_nsa(x, g, scale, shift, tt, cmp_tiles, rel_bias, w_in, cmp_w1, cmp_b1, cmp_w2, cmp_pe):
    b, s, d = x.shape
    gq, hg = NSA_GROUPS, NSA_HG
    qs = HEAD_DIM ** -0.5
    n_main = NSA_Q + 6 * NSA_KV
    w_main = jnp.concatenate([w_in[:, :NSA_Q] * qs, w_in[:, NSA_Q:n_main]], axis=1).astype(BF16)
    n_gate = NSA_HEADS * N_BRANCH
    w_gate = jnp.pad(w_in[:, n_main:], ((0, 0), (0, LANES - n_gate))).astype(BF16)
    proj, gates = _norm_proj(x, g, scale, shift, w_main, w_gate, 512, "nsa_in_proj")
    gates_t = gates[..., :n_gate].reshape(b, s, gq, hg * N_BRANCH).transpose(0, 2, 3, 1)
    gates_t = jnp.pad(gates_t, ((0, 0), (0, 0), (0, 16 - hg * N_BRANCH), (0, 0)))
    nq = s // QT
    q5 = proj[..., :NSA_Q].reshape(b, nq, QT, gq, hg, HEAD_DIM)
    qt = q5.transpose(0, 3, 1, 5, 4, 2).reshape(b, gq, nq, HEAD_DIM, hg * QT)
    part = lambda j: proj[..., NSA_Q + j * NSA_KV:NSA_Q + (j + 1) * NSA_KV]
    nc = s // CMP_STRIDE
    kvc = jnp.stack([part(0), part(1)])
    a = kvc.reshape(2, b, nc, CMP_STRIDE, gq, HEAD_DIM).transpose(0, 1, 4, 2, 3, 5)
    a = a.reshape(2, b, gq, nc, CMP_STRIDE * HEAD_DIM)
    cmp = _compress(a, cmp_w1.astype(BF16), cmp_b1, cmp_w2.astype(BF16), cmp_pe)
    oct, selb = _cattn(rel_bias, qt, cmp[0], cmp[1].transpose(0, 1, 3, 2), cmp_tiles)
    grp = lambda t: t.reshape(b, s, gq, HEAD_DIM).transpose(0, 2, 1, 3)
    onehot = (jnp.arange(s)[:, None] // SEL_BLOCK == jnp.arange(LANES)[None, :]).astype(BF16)
    ksa = jnp.concatenate([grp(part(2)), jnp.broadcast_to(onehot, (b, gq, s, LANES))], axis=-1)

    def values_t(t):
        vt = t.reshape(b, s, gq, HEAD_DIM).transpose(0, 2, 3, 1)
        ones = jnp.ones((b, gq, 1, s), BF16)
        return jnp.concatenate([vt, ones, jnp.zeros((b, gq, VROWS - HEAD_DIM - 1, s), BF16)], axis=2)

    ot = _sattn(qt, selb, ksa, values_t(part(3)), grp(part(4)), values_t(part(5)), tt, oct, gates_t)
    o = ot.reshape(b, gq, nq, HEAD_DIM, hg, QT).transpose(0, 2, 5, 1, 4, 3)
    return o.reshape(b, s, NSA_Q)


def kernel(x, c, rel_bias, ada_w, ada_b, norm_g, ab_w_in, ab_conv_w, ab_conv_b, ab_dt_bias, ab_a_log,
           ab_d_skip, ab_ssm_norm, ab_w_out, nsa_w_in, nsa_cmp_w1, nsa_cmp_b1, nsa_cmp_w2, nsa_cmp_pe,
           nsa_w_out, ffn_w_up, ffn_conv_w, ffn_conv_b, ffn_w_down):
    depth = ada_w.shape[0]
    d = x.shape[-1]
    assert x.shape[1] % (CMP_STRIDE * QT) == 0 and x.shape[1] // SEL_BLOCK <= LANES
    mod = _modulation(c, ada_w, ada_b)
    tt = _bias_tiles(rel_bias) if depth > 1 else None
    cmp_tiles = _cmp_bias_tiles(rel_bias) if depth > 1 else None
    for l in range(depth):
        shift1, scale1, gate1, shift2, scale2, gate2 = [mod[l, :, j * d:(j + 1) * d] for j in range(6)]
        if l % 2 == 0:
            e = l // 2
            o_sb, y = _mixer_ab(x, norm_g[l, 0], scale1, shift1, ab_w_in[e], ab_conv_w[e], ab_conv_b[e],
                                ab_dt_bias[e], ab_a_log[e], ab_d_skip[e], ab_ssm_norm[e])
            w_out = ab_w_out[e].astype(BF16)
            x = _out_proj([o_sb, y], [w_out[:SB_WIDTH], w_out[SB_WIDTH:]], norm_g[l, 1], gate1, x, "ab_out_proj")
        else:
            o = l // 2
            a = _mixer_nsa(x, norm_g[l, 0], scale1, shift1, tt, cmp_tiles, rel_bias, nsa_w_in[o], nsa_cmp_w1[o],
                           nsa_cmp_b1[o], nsa_cmp_w2[o], nsa_cmp_pe[o])
            x = _out_proj([a], [nsa_w_out[o].astype(BF16)], norm_g[l, 1], gate1, x, "nsa_out_proj")
        a = _ffn_up(x, norm_g[l, 2], scale2, shift2, ffn_w_up[l].astype(BF16), ffn_conv_w[l], ffn_conv_b[l])
        x = _out_proj([a], [ffn_w_down[l].astype(BF16)], norm_g[l, 3], gate2, x, "ffn_down_proj")
    return x
```

```python
import functools
import math

import numpy as np
import jax
import jax.numpy as jnp
from jax import lax
from jax.experimental import pallas as pl
from jax.experimental.pallas import tpu as pltpu

BF16 = jnp.bfloat16
F32 = jnp.float32

SB_HEADS = 8
HEAD_DIM = 64
SB_WIDTH = SB_HEADS * HEAD_DIM
SSM_HEADS = 16
SSM_INNER = SSM_HEADS * HEAD_DIM
SSM_GROUPS = 4
SSM_STATE = 128
SSM_CONV = 4
SSM_CHUNK = 128
SSM_CONV_DIM = SSM_INNER + 2 * SSM_GROUPS * SSM_STATE
NSA_HEADS = 16
NSA_GROUPS = 4
NSA_HG = NSA_HEADS // NSA_GROUPS
NSA_Q = NSA_HEADS * HEAD_DIM
NSA_KV = NSA_GROUPS * HEAD_DIM
CMP_BLOCK = 32
CMP_STRIDE = 16
SEL_BLOCK = 64
N_SEL = 16
WINDOW = 512
N_BRANCH = 3
REL_BUCKETS = 32
REL_MAX_EXACT = 16
REL_MAX_DIST = 1024
FFN_CONV = 3
EPS = 1e-6
NEG = -1e30
FORCE = 1e4
SEL_MASK = 2.0 ** 100
M_INIT = -3.0e38

QT = 128
LANES = 128
VMEM_LIMIT = 48 * 1024 * 1024


def _cparams(*sem):
    return pltpu.CompilerParams(dimension_semantics=sem, vmem_limit_bytes=VMEM_LIMIT)


def _sigmoid(x):
    return 1.0 / (1.0 + jnp.exp(-x))


def _silu(x):
    h = 0.5 * x
    return h + h * jnp.tanh(h)


def _dot(a, b):
    return jnp.dot(a, b, preferred_element_type=F32)


def _dot_nt(a, b):
    return lax.dot_general(a, b, (((1,), (1,)), ((), ())), preferred_element_type=F32)


def _rms(y, g):
    ms = jnp.mean(y * y, axis=-1, keepdims=True)
    return y * lax.rsqrt(ms + EPS) * g


def _mod_kernel(c_ref, w_ref, b_ref, o_ref):
    c = c_ref[...]
    s = c * _sigmoid(c)
    o_ref[0] = _dot(s.astype(BF16), w_ref[0].astype(BF16)) + b_ref[0]


def _modulation(c, ada_w, ada_b):
    depth, d, n = ada_w.shape
    b = c.shape[0]
    bp = 8
    tn = 768
    cp = jnp.pad(c, ((0, bp - b), (0, 0)))
    out = pl.pallas_call(
        _mod_kernel,
        grid=(depth, n // tn),
        in_specs=[pl.BlockSpec((bp, d), lambda l, j: (0, 0)),
                  pl.BlockSpec((1, d, tn), lambda l, j: (l, 0, j)),
                  pl.BlockSpec((1, 1, tn), lambda l, j: (l, 0, j))],
        out_specs=pl.BlockSpec((1, bp, tn), lambda l, j: (l, 0, j)),
        out_shape=jax.ShapeDtypeStruct((depth, bp, n), F32),
        compiler_params=_cparams("parallel", "parallel"),
        name="adaln_mod",
    )(cp, ada_w, ada_b.reshape(depth, 1, n))
    return out[:, :b]


def _norm_proj_kernel(x_ref, g_ref, sc_ref, sh_ref, w_ref, w2_ref, o_ref, o2_ref, *, tn):
    y = _rms(x_ref[0], g_ref[...])
    h = (y * (1.0 + sc_ref[0]) + sh_ref[0]).astype(BF16)
    for j in range(w_ref.shape[1] // tn):
        o_ref[0, :, j * tn:(j + 1) * tn] = _dot(h, w_ref[:, j * tn:(j + 1) * tn]).astype(o_ref.dtype)
    o2_ref[0] = _dot(h, w2_ref[...])


def _norm_proj(x, g, scale, shift, w, w_f32, tn, name):
    b, s, d = x.shape
    n = w.shape[1]
    tm = min(512, s)
    return pl.pallas_call(
        functools.partial(_norm_proj_kernel, tn=tn),
        grid=(b, s // tm),
        in_specs=[pl.BlockSpec((1, tm, d), lambda bi, i: (bi, i, 0)),
                  pl.BlockSpec((1, d), lambda bi, i: (0, 0)),
                  pl.BlockSpec((1, 1, d), lambda bi, i: (bi, 0, 0)),
                  pl.BlockSpec((1, 1, d), lambda bi, i: (bi, 0, 0)),
                  pl.BlockSpec((d, n), lambda bi, i: (0, 0)),
                  pl.BlockSpec((d, LANES), lambda bi, i: (0, 0))],
        out_specs=[pl.BlockSpec((1, tm, n), lambda bi, i: (bi, i, 0)),
                   pl.BlockSpec((1, tm, LANES), lambda bi, i: (bi, i, 0))],
        out_shape=[jax.ShapeDtypeStruct((b, s, n), BF16), jax.ShapeDtypeStruct((b, s, LANES), F32)],
        compiler_params=_cparams("parallel", "parallel"),
        name=name,
    )(x, g.reshape(1, d), scale.reshape(b, 1, d), shift.reshape(b, 1, d), w, w_f32)


def _out_proj_kernel(*refs, n_in):
    a_refs = refs[:n_in]
    w_refs = refs[n_in:2 * n_in]
    g_ref, gate_ref, x_ref, o_ref = refs[2 * n_in:]
    acc = _dot(a_refs[0][0], w_refs[0][...])
    for a_ref, w_ref in zip(a_refs[1:], w_refs[1:]):
        acc = acc + _dot(a_ref[0], w_ref[...])
    o_ref[0] = x_ref[0] + gate_ref[0] * _rms(acc, g_ref[...])


def _out_proj(acts, ws, g, gate, x, name):
    b, s, d = x.shape
    tm = min(512, s)
    n_in = len(acts)
    in_specs = [pl.BlockSpec((1, tm, a.shape[2]), lambda bi, i: (bi, i, 0)) for a in acts]
    in_specs += [pl.BlockSpec(w.shape, lambda bi, i: (0, 0)) for w in ws]
    in_specs += [pl.BlockSpec((1, d), lambda bi, i: (0, 0)),
                 pl.BlockSpec((1, 1, d), lambda bi, i: (bi, 0, 0)),
                 pl.BlockSpec((1, tm, d), lambda bi, i: (bi, i, 0))]
    return pl.pallas_call(
        functools.partial(_out_proj_kernel, n_in=n_in),
        grid=(b, s // tm),
        in_specs=in_specs,
        out_specs=pl.BlockSpec((1, tm, d), lambda bi, i: (bi, i, 0)),
        out_shape=jax.ShapeDtypeStruct((b, s, d), F32),
        compiler_params=_cparams("parallel", "parallel"),
        name=name,
    )(*acts, *ws, g.reshape(1, d), gate.reshape(b, 1, d), x)


def _shift_rows(cur, prev, k):
    sc = pltpu.roll(cur, k, 0)
    sp = pltpu.roll(prev, k, 0)
    row8 = lax.broadcasted_iota(jnp.int32, sp.shape, 0)
    head = jnp.where(row8 < k, sp, sc[:8])
    return jnp.concatenate([head, sc[8:]], axis=0)


def _gelu_tanh_x2(x):
    c = math.sqrt(2.0 / math.pi)
    return x * (1.0 + jnp.tanh(x * (c + (c * 0.044715) * (x * x))))


def _ffn_up_kernel(x_ref, xp_ref, g_ref, sc_ref, sh_ref, w_ref, cw_ref, cb_ref, o_ref, u_ref, *, dff, ck):
    def modulated(xv):
        return (_rms(xv, g_ref[...]) * (1.0 + sc_ref[0]) + sh_ref[0]).astype(BF16)

    tm = x_ref.shape[1]
    h = modulated(x_ref[0])
    hp = modulated(xp_ref[0])
    first = pl.program_id(1) == 0

    def conv(slot, col0):
        w_up = w_ref[:, col0:col0 + ck]
        cur = _dot(h, w_up)
        u_ref[slot, 0:8, :] = jnp.where(first, 0.0, _dot(hp, w_up))
        u_ref[slot, 8:8 + tm, :] = cur
        w = cw_ref[:, col0:col0 + ck]
        y = w[2:3] * cur + cb_ref[:, col0:col0 + ck]
        y = y + w[1:2] * u_ref[slot, 7:7 + tm, :]
        return y + w[0:1] * u_ref[slot, 6:6 + tm, :]

    for c in range(dff // ck):
        act = _gelu_tanh_x2(conv(0, c * ck)) * conv(1, dff + c * ck)
        o_ref[0, :, c * ck:(c + 1) * ck] = act.astype(o_ref.dtype)


def _ffn_up(x, g, scale, shift, w_up, conv_w, conv_b):
    b, s, d = x.shape
    c2 = w_up.shape[1]
    dff = c2 // 2
    tm = min(512, s)
    half = jnp.concatenate([jnp.ones((dff,), F32), jnp.full((dff,), 0.5, F32)])
    cw = jnp.pad(conv_w * half, ((0, 8 - conv_w.shape[0]), (0, 0)))
    conv_b = conv_b * half
    return pl.pallas_call(
        functools.partial(_ffn_up_kernel, dff=dff, ck=256),
        grid=(b, s // tm),
        in_specs=[pl.BlockSpec((1, tm, d), lambda bi, i: (bi, i, 0)),
                  pl.BlockSpec((1, 8, d), lambda bi, i: (bi, jnp.maximum(i * (tm // 8) - 1, 0), 0)),
                  pl.BlockSpec((1, d), lambda bi, i: (0, 0)),
                  pl.BlockSpec((1, 1, d), lambda bi, i: (bi, 0, 0)),
                  pl.BlockSpec((1, 1, d), lambda bi, i: (bi, 0, 0)),
                  pl.BlockSpec((d, c2), lambda bi, i: (0, 0)),
                  pl.BlockSpec((8, c2), lambda bi, i: (0, 0)),
                  pl.BlockSpec((1, c2), lambda bi, i: (0, 0))],
        out_specs=pl.BlockSpec((1, tm, dff), lambda bi, i: (bi, i, 0)),
        out_shape=jax.ShapeDtypeStruct((b, s, dff), BF16),
        scratch_shapes=[pltpu.VMEM((2, 8 + tm, 256), F32)],
        compiler_params=_cparams("parallel", "parallel"),
        name="ffn_up_conv_act",
    )(x, x, g.reshape(1, d), scale.reshape(b, 1, d), shift.reshape(b, 1, d), w_up, cw, conv_b.reshape(1, c2))


SBQ = 512


SB_TILES = 4
SB_DEAD = -104.0


def _sb_kernel(q_ref, k_ref, v_ref, u_ref, o_ref, acc_ref, cs_ref):
    i = pl.program_id(2)
    qp = q_ref[0]
    lane_lo = lax.broadcasted_iota(jnp.int32, (SBQ, LANES), 1) < HEAD_DIM
    zero = jnp.zeros_like(qp)
    qs = [jnp.where(lane_lo, qp, zero), jnp.where(lane_lo, zero, qp)]
    nsub = SBQ // QT
    diff = (lax.broadcasted_iota(jnp.int32, (SBQ, QT), 1) - lax.broadcasted_iota(jnp.int32, (SBQ, QT), 0))
    acc_ref[...] = jnp.zeros(acc_ref.shape, F32)
    cs_ref[...] = jnp.zeros(cs_ref.shape, F32)

    def tiles(kidxs, diag):
        offs = [pl.multiple_of(kidx * QT, QT) for kidx in kidxs]
        ks = [k_ref[0, pl.ds(k0, QT), :] for k0 in offs]
        zs = [[_dot_nt(q, k) for k in ks] for q in qs]
        css = [[], []]
        belows = [diff < i * SBQ - kidx * QT for kidx in kidxs] if diag else None
        for h in range(2):
            for t, z in enumerate(zs[h]):
                nl = jnp.maximum(z, 0.0) + jnp.log(1.0 + jnp.exp(-jnp.abs(z)))
                if diag:
                    nl = jnp.where(belows[t], nl, 0.0)
                hi = nl.astype(BF16)
                lo = (nl - hi.astype(F32)).astype(BF16)
                css[h].append(_dot(jnp.concatenate([hi, lo], axis=1), u_ref[...]))
        vs = [v_ref[0, pl.ds(k0, QT), :] for k0 in offs]
        for h in range(2):
            csum = cs_ref[h]
            acc = acc_ref[h]
            for t, (z, cs) in enumerate(zip(zs[h], css[h])):
                w = jnp.exp(z + cs[:, :QT] + csum)
                if diag:
                    w = jnp.where(belows[t], w, 0.0)
                acc = acc + _dot(w.astype(BF16), vs[t])
                csum = csum + cs[:, QT:]
            acc_ref[h] = acc
            cs_ref[h] = csum

    for r0 in range(nsub - 1, -1, -SB_TILES):
        tiles([i * nsub + r0 - r for r in range(SB_TILES)], True)

    n_trips = i * (nsub // SB_TILES)

    def cond(carry):
        it, cmax = carry
        return jnp.logical_and(it < n_trips, cmax > SB_DEAD)

    def body(carry):
        it, _ = carry
        tiles([i * nsub - 1 - it * SB_TILES - r for r in range(SB_TILES)], False)
        return it + 1, jnp.max(cs_ref[...])

    lax.while_loop(cond, body, (jnp.int32(0), jnp.max(cs_ref[...])))
    o_ref[0] = jnp.where(lane_lo, acc_ref[0], acc_ref[1]).astype(o_ref.dtype)


def _suffix_matrix():
    j = np.arange(2 * QT)[:, None] % QT
    s = np.arange(QT + LANES)[None, :]
    return jnp.asarray(-np.where(s < QT, j >= s, True).astype(np.float32), dtype=BF16)


def _sb_attention(proj, q_col, k_col, v_col):
    b, s, _ = proj.shape
    pairs = SB_HEADS // 2
    qb, kb, vb = q_col // LANES, k_col // LANES, v_col // LANES
    return pl.pallas_call(
        _sb_kernel,
        grid=(b, pairs, s // SBQ),
        in_specs=[pl.BlockSpec((1, SBQ, LANES), lambda bi, p, i: (bi, i, qb + p)),
                  pl.BlockSpec((1, s, LANES), lambda bi, p, i: (bi, 0, kb + p)),
                  pl.BlockSpec((1, s, LANES), lambda bi, p, i: (bi, 0, vb + p)),
                  pl.BlockSpec((2 * QT, QT + LANES), lambda bi, p, i: (0, 0))],
        out_specs=pl.BlockSpec((1, SBQ, LANES), lambda bi, p, i: (bi, i, p)),
        out_shape=jax.ShapeDtypeStruct((b, s, SB_WIDTH), BF16),
        scratch_shapes=[pltpu.VMEM((2, SBQ, LANES), F32), pltpu.VMEM((2, SBQ, LANES), F32)],
        compiler_params=_cparams("parallel", "parallel", "arbitrary"),
        name="sb_attention",
    )(proj, proj, proj, _suffix_matrix())


def _pair(f, p, lane_lo):
    return jnp.where(lane_lo, f[:, 2 * p:2 * p + 1], f[:, 2 * p + 1:2 * p + 2])


def _ssd_kernel(xr_ref, xp_ref, dt_ref, z_ref, cw_ref, cb_ref, dtb_ref, alog_ref, dsk_ref, nrm_ref,
                o_ref, st_ref, xc_ref, y_ref):
    ln = SSM_CHUNK
    c = pl.program_id(1)

    @pl.when(c == 0)
    def _():
        st_ref[...] = jnp.zeros(st_ref.shape, F32)

    first = c == 0
    ck = 256
    for j in range(SSM_CONV_DIM // ck):
        cur = xr_ref[0, :, j * ck:(j + 1) * ck].astype(F32)
        prev = xp_ref[0, :, j * ck:(j + 1) * ck].astype(F32)
        prev = jnp.where(first, 0.0, prev)
        w = cw_ref[:, j * ck:(j + 1) * ck]
        y = w[3:4] * cur + cb_ref[:, j * ck:(j + 1) * ck]
        for kk in range(1, SSM_CONV):
            y = y + w[3 - kk:4 - kk] * _shift_rows(cur, prev, kk)
        xc_ref[:, j * ck:(j + 1) * ck] = _silu(y)

    dtr = dt_ref[0] + dtb_ref[...]
    dt = jnp.maximum(dtr, 0.0) + jnp.log(1.0 + jnp.exp(-jnp.abs(dtr)))
    a = dt * (-jnp.exp(alog_ref[...]))
    row = lax.broadcasted_iota(jnp.int32, (ln, ln), 0)
    col = lax.broadcasted_iota(jnp.int32, (ln, ln), 1)
    causal = col <= row
    acs = jnp.dot(causal.astype(F32), a, precision=lax.Precision.HIGHEST, preferred_element_type=F32)
    acs_t = acs.T
    eacs = jnp.exp(acs)
    dout = jnp.exp(acs[ln - 1:ln, :] - acs)
    lane_lo = lax.broadcasted_iota(jnp.int32, (ln, LANES), 1) < HEAD_DIM

    hg = SSM_HEADS // SSM_GROUPS
    for g in range(SSM_GROUPS):
        bm = xc_ref[:, SSM_INNER + g * SSM_STATE:SSM_INNER + (g + 1) * SSM_STATE]
        cm = xc_ref[:, SSM_INNER + (SSM_GROUPS + g) * SSM_STATE:SSM_INNER + (SSM_GROUPS + g + 1) * SSM_STATE]
        cmb = cm.astype(BF16)
        cb = _dot_nt(cmb, bm.astype(BF16))
        prev = st_ref[g]
        yoff = _dot(cmb, prev.astype(BF16))
        dox_parts = []
        cdec_parts = []
        for pp in range(hg // 2):
            p = g * (hg // 2) + pp
            xs = xc_ref[:, p * LANES:(p + 1) * LANES]
            xdt = xs * _pair(dt, p, lane_lo)
            ms = []
            for hh in (2 * p, 2 * p + 1):
                seg = acs[:, hh:hh + 1] - acs_t[hh:hh + 1, :]
                decay = jnp.exp(jnp.where(causal, seg, NEG))
                ms.append((cb * decay).astype(BF16))
            rhs = jnp.concatenate([jnp.where(lane_lo, xdt, 0.0), jnp.where(lane_lo, 0.0, xdt)], axis=0)
            ydiag = _dot(jnp.concatenate(ms, axis=1), rhs.astype(BF16))
            yo = yoff[:, pp * LANES:(pp + 1) * LANES] * _pair(eacs, p, lane_lo)
            y_ref[:, p * LANES:(p + 1) * LANES] = ydiag + yo + xs * dsk_ref[:, p * LANES:(p + 1) * LANES]
            dox_parts.append((xdt * _pair(dout, p, lane_lo)).astype(BF16))
            cdec_parts.append(_pair(eacs[ln - 1:ln, :], p, lane_lo[0:1]))
        states = _dot(bm.T.astype(BF16), jnp.concatenate(dox_parts, axis=1))
        st_ref[g] = prev * jnp.concatenate(cdec_parts, axis=1) + states

    zf = z_ref[0].astype(F32)
    yz = y_ref[...] * _silu(zf)
    o_ref[0] = _rms(yz, nrm_ref[...]).astype(o_ref.dtype)


def _ssd(proj, dt_raw, conv_w, conv_b, dt_bias, a_log, d_skip, ssm_norm):
    b, s, _ = proj.shape
    ln = SSM_CHUNK
    cw = jnp.pad(conv_w, ((0, 8 - SSM_CONV), (0, 0)))
    pad = LANES - SSM_HEADS
    return pl.pallas_call(
        _ssd_kernel,
        grid=(b, s // ln),
        in_specs=[pl.BlockSpec((1, ln, SSM_CONV_DIM), lambda bi, c: (bi, c, 0)),
                  pl.BlockSpec((1, 8, SSM_CONV_DIM), lambda bi, c: (bi, jnp.maximum(c * (ln // 8) - 1, 0), 0)),
                  pl.BlockSpec((1, ln, LANES), lambda bi, c: (bi, c, 0)),
                  pl.BlockSpec((1, ln, SSM_INNER), lambda bi, c: (bi, c, SSM_CONV_DIM // SSM_INNER)),
                  pl.BlockSpec((8, SSM_CONV_DIM), lambda bi, c: (0, 0)),
                  pl.BlockSpec((1, SSM_CONV_DIM), lambda bi, c: (0, 0)),
                  pl.BlockSpec((1, LANES), lambda bi, c: (0, 0)),
                  pl.BlockSpec((1, LANES), lambda bi, c: (0, 0)),
                  pl.BlockSpec((1, SSM_INNER), lambda bi, c: (0, 0)),
                  pl.BlockSpec((1, SSM_INNER), lambda bi, c: (0, 0))],
        out_specs=pl.BlockSpec((1, ln, SSM_INNER), lambda bi, c: (bi, c, 0)),
        out_shape=jax.ShapeDtypeStruct((b, s, SSM_INNER), BF16),
        scratch_shapes=[pltpu.VMEM((SSM_GROUPS, SSM_STATE, SSM_INNER // SSM_GROUPS), F32),
                        pltpu.VMEM((ln, SSM_CONV_DIM), F32),
                        pltpu.VMEM((ln, SSM_INNER), F32)],
        compiler_params=_cparams("parallel", "arbitrary"),
        name="ssd",
    )(proj, proj, dt_raw, proj, cw, conv_b.reshape(1, -1),
      jnp.pad(dt_bias, (0, pad)).reshape(1, LANES), jnp.pad(a_log, (0, pad)).reshape(1, LANES),
      jnp.repeat(d_skip, HEAD_DIM).reshape(1, SSM_INNER), ssm_norm.reshape(1, SSM_INNER))


def _rel_bucket(n):
    n = jnp.maximum(n, 0)
    nf = jnp.maximum(n, 1).astype(F32)
    large = REL_MAX_EXACT + (jnp.log(nf / REL_MAX_EXACT) / math.log(REL_MAX_DIST / REL_MAX_EXACT)
                             * (REL_BUCKETS - REL_MAX_EXACT)).astype(jnp.int32)
    large = jnp.minimum(large, REL_BUCKETS - 1)
    return jnp.where(n < REL_MAX_EXACT, n, large)


def _bias_lookup(rb_ref, head, bucket):
    out = jnp.full(bucket.shape, rb_ref[head, 0], F32)
    for kk in range(1, REL_BUCKETS):
        out = jnp.where(bucket >= kk, rb_ref[head, kk], out)
    return out


N_NEAR = 8


def _bias_tiles_kernel(rb_ref, o_ref):
    h = pl.program_id(0)
    row = lax.broadcasted_iota(jnp.int32, (QT, QT), 0)
    col = lax.broadcasted_iota(jnp.int32, (QT, QT), 1)
    for d in range(N_NEAR):
        bucket = _rel_bucket(d * QT + col - row)
        o_ref[0, d] = _bias_lookup(rb_ref, h, bucket) - rb_ref[h, REL_BUCKETS - 1]


def _bias_tiles(rel_bias):
    return pl.pallas_call(
        _bias_tiles_kernel,
        grid=(NSA_HEADS,),
        in_specs=[pl.BlockSpec(memory_space=pltpu.SMEM)],
        out_specs=pl.BlockSpec((1, N_NEAR, QT, QT), lambda h: (h, 0, 0, 0)),
        out_shape=jax.ShapeDtypeStruct((NSA_HEADS, N_NEAR, QT, QT), F32),
        compiler_params=_cparams("parallel"),
        name="nsa_bias_tiles",
    )(rel_bias)


def _compress_kernel(a_ref, w1_ref, b1_ref, w2_ref, pe_ref, o_ref):
    a = a_ref[0, 0, 0]
    ncp = a.shape[0]
    half = CMP_STRIDE * HEAD_DIM
    h1 = _dot(a, w1_ref[0, :half, :])
    h2 = _dot(a, w1_ref[0, half:, :])
    h2 = pltpu.roll(h2, ncp - 1, 0)
    rowi = lax.broadcasted_iota(jnp.int32, h2.shape, 0)
    h2 = jnp.where(rowi < ncp - 1, h2, 0.0)
    pev = _dot(pe_ref[0], w1_ref[0])[0:1]
    hid = h1 + h2 + pev + b1_ref[0]
    act = hid * _sigmoid(hid)
    o_ref[0, 0, 0] = _dot(act.astype(BF16), w2_ref[0]).astype(o_ref.dtype)


def _compress(a, w1, b1, w2, pe):
    _, b, g, ncp, kk = a.shape
    hid = w1.shape[2]
    pe8 = jnp.pad(pe.reshape(2, 1, CMP_BLOCK * HEAD_DIM), ((0, 0), (0, 7), (0, 0))).astype(BF16)
    return pl.pallas_call(
        _compress_kernel,
        grid=(2, b, g),
        in_specs=[pl.BlockSpec((1, 1, 1, ncp, kk), lambda t, bi, gi: (t, bi, gi, 0, 0)),
                  pl.BlockSpec((1, 2 * kk, hid), lambda t, bi, gi: (t, 0, 0)),
                  pl.BlockSpec((1, 1, hid), lambda t, bi, gi: (t, 0, 0)),
                  pl.BlockSpec((1, hid, HEAD_DIM), lambda t, bi, gi: (t, 0, 0)),
                  pl.BlockSpec((1, 8, 2 * kk), lambda t, bi, gi: (t, 0, 0))],
        out_specs=pl.BlockSpec((1, 1, 1, ncp, HEAD_DIM), lambda t, bi, gi: (t, bi, gi, 0, 0)),
        out_shape=jax.ShapeDtypeStruct((2, b, g, ncp, HEAD_DIM), BF16),
        compiler_params=_cparams("parallel", "parallel", "parallel"),
        name="nsa_compress",
    )(a, w1, b1.reshape(2, 1, hid), w2, pe8)


BUCKET_CONST_FROM = 800
CMP_SPAN = CMP_STRIDE * QT
CMP_TILE_REACH = CMP_STRIDE * (QT - 1) + CMP_BLOCK - 1
N_CMP_TILES = -(-(BUCKET_CONST_FROM + CMP_TILE_REACH) // QT)


def _cmp_bias_tiles_kernel(rb_ref, o_ref):
    h = pl.program_id(0)
    t0 = lax.broadcasted_iota(jnp.int32, (QT, QT), 1)
    cend = lax.broadcasted_iota(jnp.int32, (QT, QT), 0) * CMP_STRIDE + (CMP_BLOCK - 1)
    for k in range(N_CMP_TILES):
        t = t0 + k * QT
        o_ref[0, k] = jnp.where(cend <= t, _bias_lookup(rb_ref, h, _rel_bucket(t - cend)), NEG)


def _cmp_bias_tiles(rel_bias):
    return pl.pallas_call(
        _cmp_bias_tiles_kernel,
        grid=(NSA_HEADS,),
        in_specs=[pl.BlockSpec(memory_space=pltpu.SMEM)],
        out_specs=pl.BlockSpec((1, N_CMP_TILES, QT, QT), lambda h: (h, 0, 0, 0)),
        out_shape=jax.ShapeDtypeStruct((NSA_HEADS, N_CMP_TILES, QT, QT), F32),
        compiler_params=_cparams("parallel"),
        name="nsa_cmp_bias_tiles",
    )(rel_bias)


def _cattn_kernel(rb_ref, qt_ref, k_ref, vt_ref, ov_ref, tab_ref, oc_ref, sel_ref, bias_ref):
    g = pl.program_id(0)
    i = pl.program_id(1)
    nb = k_ref.shape[0]
    ncp = k_ref.shape[2]
    q0 = i * QT
    for r in range(ncp // QT):
        rows = slice(r * QT, (r + 1) * QT)
        min_rel = q0 - (CMP_STRIDE * (r * QT + QT - 1) + CMP_BLOCK - 1)
        max_rel = q0 + QT - 1 - (CMP_STRIDE * r * QT + CMP_BLOCK - 1)

        @pl.when((min_rel < BUCKET_CONST_FROM) & (max_rel >= 0))
        def _(r=r, rows=rows):
            k = i - r * (CMP_SPAN // QT)
            for h in range(NSA_HG):
                bias_ref[h, rows, :] = tab_ref[h, k]

        @pl.when(min_rel >= BUCKET_CONST_FROM)
        def _(rows=rows):
            for h in range(NSA_HG):
                bias_ref[h, rows, :] = jnp.full((QT, QT), rb_ref[g * NSA_HG + h, REL_BUCKETS - 1], F32)

        @pl.when(max_rel < 0)
        def _(rows=rows):
            for h in range(NSA_HG):
                bias_ref[h, rows, :] = jnp.full((QT, QT), NEG, F32)

    anyvalid = q0 + lax.broadcasted_iota(jnp.int32, (1, QT), 1) >= CMP_BLOCK - 1
    qks = [_dot(k_ref[bi, 0], qt_ref[bi, 0, 0]) for bi in range(nb)]
    pcsums = []
    for bi in range(nb):
        vt = vt_ref[bi, 0]
        pcsum = jnp.zeros((ncp, QT), F32)
        ocs = []
        for h in range(NSA_HG):
            lg = qks[bi][:, h * QT:(h + 1) * QT] + bias_ref[h]
            m = jnp.max(lg, axis=0, keepdims=True)
            e = jnp.exp(lg - m)
            p = e * jnp.where(anyvalid, 1.0 / jnp.sum(e, axis=0, keepdims=True), 0.0)
            pcsum = pcsum + p
            ocs.append(_dot(vt, p.astype(BF16)))
        oc_ref[bi, 0, 0] = jnp.concatenate(ocs, axis=1).astype(oc_ref.dtype)
        pcsums.append(pcsum)

    imps = []
    for pcsum in pcsums:
        hi = pcsum.astype(BF16)
        lo = (pcsum - hi.astype(F32)).astype(BF16)
        imps.append(_dot(ov_ref[...], hi) + _dot(ov_ref[...], lo))
    jrow = lax.broadcasted_iota(jnp.int32, (LANES, QT), 0)
    tq = q0 + lax.broadcasted_iota(jnp.int32, (LANES, QT), 1)
    cur = jnp.right_shift(tq, SEL_BLOCK.bit_length() - 1)
    forced = (jrow == 0) | (jrow == cur) | (jrow == cur - 1)
    jf = jrow.astype(F32)
    for bi, imp in enumerate(imps):
        val = jnp.where(forced, FORCE, jnp.where(jrow * SEL_BLOCK <= tq, imp, -1.0))
        sel = jnp.zeros((LANES, QT), F32)
        for _ in range(N_SEL):
            m = jnp.max(val, axis=0, keepdims=True)
            jm = jnp.min(jnp.where(val == m, jf, 1e9), axis=0, keepdims=True)
            hit = jf == jm
            sel = jnp.where(hit, 1.0, sel)
            val = jnp.where(hit, M_INIT, val)
        sel_ref[bi, 0, 0] = ((sel - 1.0) * SEL_MASK).astype(sel_ref.dtype)


def _overlap_t(ncp):
    ci = (np.arange(ncp) * CMP_STRIDE)[None, :]
    sj = (np.arange(LANES) * SEL_BLOCK)[:, None]
    return jnp.asarray(((ci < sj + SEL_BLOCK) & (ci + CMP_BLOCK > sj)).astype(np.float32), dtype=BF16)


def _cattn(rel_bias, qt, kcmp, vcmp_t, cmp_tiles):
    b, g, nq, d, cols = qt.shape
    ncp = kcmp.shape[2]
    hg = NSA_HG
    q_spec = pl.BlockSpec((b, 1, 1, d, cols), lambda gi, i: (0, gi, i, 0, 0))
    return pl.pallas_call(
        _cattn_kernel,
        grid=(g, nq),
        in_specs=[pl.BlockSpec(memory_space=pltpu.SMEM),
                  q_spec,
                  pl.BlockSpec((b, 1, ncp, d), lambda gi, i: (0, gi, 0, 0)),
                  pl.BlockSpec((b, 1, d, ncp), lambda gi, i: (0, gi, 0, 0)),
                  pl.BlockSpec((LANES, ncp), lambda gi, i: (0, 0)),
                  pl.BlockSpec((hg, N_CMP_TILES, QT, QT), lambda gi, i: (gi, 0, 0, 0))],
        out_specs=[q_spec,
                   pl.BlockSpec((b, 1, 1, LANES, QT), lambda gi, i: (0, gi, i, 0, 0))],
        out_shape=[jax.ShapeDtypeStruct((b, g, nq, d, cols), BF16),
                   jax.ShapeDtypeStruct((b, g, nq, LANES, QT), BF16)],
        scratch_shapes=[pltpu.VMEM((hg, ncp, QT), F32)],
        compiler_params=_cparams("parallel", "parallel"),
        name="nsa_cmp_attn",
    )(rel_bias, qt, kcmp, vcmp_t, _overlap_t(ncp), cmp_tiles)


VROWS = 80


def _sattn_kernel(qt_ref, sel_ref, ks_ref, vs_ref, kw_ref, vw_ref, tt_ref, oc_ref, gt_ref, o_ref,
                  m_ref, acc_ref, mw_ref, accw_ref, s0_ref, s1_ref):
    i = pl.program_id(2)
    cols = NSA_HG * QT
    qt = qt_ref[0, 0, 0]
    qa = jnp.concatenate([qt, jnp.concatenate([sel_ref[0, 0, 0]] * NSA_HG, axis=1)], axis=0)
    krow = lax.broadcasted_iota(jnp.int32, (QT, cols), 0)
    qcol = lax.broadcasted_iota(jnp.int32, (QT, cols), 1) & (QT - 1)
    sel_state = (m_ref, acc_ref)
    win_state = (mw_ref, accw_ref)

    for mr, ar in (sel_state, win_state):
        mr[...] = jnp.full(mr.shape, M_INIT, F32)
        ar[...] = jnp.zeros(ar.shape, F32)

    def update_state(state, *tiles):
        mr, ar = state
        m = mr[...]
        acc = ar[...]
        for s, vt in tiles:
            m_new = jnp.maximum(m, jnp.max(s, axis=0, keepdims=True))
            p = jnp.exp(s - m_new[0:1])
            acc = acc * jnp.exp(m - m_new)[0:1] + _dot(vt, p.astype(BF16))
            m = m_new
        ar[...] = acc
        mr[...] = m

    def update(*tiles):
        update_state(sel_state, *tiles)

    def near_bias(d):
        return jnp.concatenate([tt_ref[h, d] for h in range(NSA_HG)], axis=1)

    def finish(state):
        acc = state[1][...]
        return acc[:HEAD_DIM] / acc[HEAD_DIM:HEAD_DIM + 1]

    def sel_tile(kt, d, size=QT):
        k0 = pl.multiple_of(kt * QT, QT)
        s = _dot(ks_ref[0, 0, pl.ds(k0, size), :], qa)
        if d is not None:
            s = s + near_bias(d)
        return s, vs_ref[0, 0, :, pl.ds(k0, size)]

    def group(k_ref, v_ref, q_op, kt0, ds, first_gt, last_le):
        n = len(ds)
        k0 = pl.multiple_of(kt0 * QT, QT)
        s = _dot(k_ref[0, 0, pl.ds(k0, n * QT), :], q_op)
        parts = []
        for t, d in enumerate(ds):
            blk = s[t * QT:(t + 1) * QT] + near_bias(d)
            if t == 0 and first_gt:
                blk = jnp.where(krow > qcol, blk, NEG)
            if t == n - 1 and last_le:
                blk = jnp.where(krow <= qcol, blk, NEG)
            parts.append(blk)
        return jnp.concatenate(parts, axis=0), v_ref[0, 0, :, pl.ds(k0, n * QT)]

    big = (N_NEAR // 2) * QT
    nwin = WINDOW // QT

    def window_tile():
        return group(kw_ref, vw_ref, qt, i - nwin, list(range(nwin, -1, -1)), True, True)

    @pl.when(i >= N_NEAR - 1)
    def _():
        n_far = i - (N_NEAR - 1)
        n_chunks = (n_far + N_NEAR - 1) // N_NEAR

        def start(c):
            return jnp.maximum(n_far - N_NEAR * (n_chunks - c), 0)

        def logits(c, dst_ref):
            for hf in range(2):
                k0 = pl.multiple_of(start(c) * QT + hf * big, QT)
                dst_ref[hf] = _dot(ks_ref[0, 0, pl.ds(k0, big), :], qa)

        def values(c, hf):
            k0 = pl.multiple_of(start(c) * QT + hf * big, QT)
            return vs_ref[0, 0, :, pl.ds(k0, big)]

        key_lane = lax.broadcasted_iota(jnp.int32, (VROWS, big), 1)

        def far_step(c, src_ref, dst_ref):
            logits(c + 1, dst_ref)
            own = (start(c + 1) - start(c)) * QT
            tiles = []
            for hf in range(2):
                vt = values(c, hf)
                tiles.append((src_ref[hf], jnp.where(key_lane < own - hf * big, vt, jnp.zeros_like(vt))))
            update(*tiles)

        def near_step(src_ref):
            win = window_tile()
            tiles = []
            for hf in range(2):
                parts = []
                for t in range(N_NEAR // 2):
                    d = N_NEAR - 1 - (hf * (N_NEAR // 2) + t)
                    blk = src_ref[hf, t * QT:(t + 1) * QT, :] + near_bias(d)
                    if d == 0:
                        blk = jnp.where(krow <= qcol, blk, NEG)
                    parts.append(blk)
                tiles.append((jnp.concatenate(parts, axis=0), values(n_chunks, hf)))
            update(*tiles)
            update_state(win_state, win)

        logits(0, s0_ref)

        def far_body(j, carry):
            far_step(2 * j, s0_ref, s1_ref)
            far_step(2 * j + 1, s1_ref, s0_ref)
            return carry

        lax.fori_loop(0, n_chunks // 2, far_body, 0)

        @pl.when(n_chunks % 2 == 1)
        def _():
            far_step(n_chunks - 1, s0_ref, s1_ref)
            near_step(s1_ref)

        @pl.when(n_chunks % 2 == 0)
        def _():
            near_step(s0_ref)

    def near_body(kt, carry):
        update(sel_tile(kt, i - kt))
        return carry

    @pl.when(i < N_NEAR - 1)
    def _():
        lax.fori_loop(0, i, near_body, 0)
        s, vt = sel_tile(i, 0)
        update((jnp.where(krow <= qcol, s, NEG), vt))

        @pl.when(i >= nwin)
        def _():
            update_state(win_state, window_tile())

        @pl.when(i < nwin)
        def _():
            for d in range(nwin - 1, -1, -1):
                @pl.when(i - d >= 0)
                def _(d=d):
                    update_state(win_state, group(kw_ref, vw_ref, qt, i - d, [d], False, d == 0))

    osl = finish(sel_state)
    ow = finish(win_state)

    gt = _sigmoid(gt_ref[0, 0])
    gate = lambda br: jnp.concatenate([gt[3 * h + br:3 * h + br + 1] for h in range(NSA_HG)], axis=1)
    o = gate(0) * oc_ref[0, 0, 0].astype(F32) + gate(1) * osl + gate(2) * ow
    o_ref[0, 0, 0] = o.astype(o_ref.dtype)


def _sattn(qt, selb, ksa, vst, kw, vwt, tt, oct, gates_t):
    b, g, nq, d, cols = qt.shape
    s = kw.shape[2]
    hg = NSA_HG
    k_spec = lambda width: pl.BlockSpec((1, 1, s, width), lambda bi, gi, i: (bi, gi, 0, 0))
    v_spec = pl.BlockSpec((1, 1, VROWS, s), lambda bi, gi, i: (bi, gi, 0, 0))
    q_spec = pl.BlockSpec((1, 1, 1, d, cols), lambda bi, gi, i: (bi, gi, i, 0, 0))
    return pl.pallas_call(
        _sattn_kernel,
        grid=(b, g, nq),
        in_specs=[q_spec,
                  pl.BlockSpec((1, 1, 1, LANES, QT), lambda bi, gi, i: (bi, gi, i, 0, 0)),
                  k_spec(ksa.shape[3]), v_spec, k_spec(d), v_spec,
                  pl.BlockSpec((hg, N_NEAR, QT, QT), lambda bi, gi, i: (gi, 0, 0, 0)),
                  q_spec,
                  pl.BlockSpec((1, 1, 16, QT), lambda bi, gi, i: (bi, gi, 0, i))],
        out_specs=q_spec,
        out_shape=jax.ShapeDtypeStruct((b, g, nq, d, cols), BF16),
        scratch_shapes=[pltpu.VMEM((8, cols), F32),
                        pltpu.VMEM((VROWS, cols), F32),
                        pltpu.VMEM((8, cols), F32),
                        pltpu.VMEM((VROWS, cols), F32),
                        pltpu.VMEM((2, (N_NEAR // 2) * QT, cols), F32),
                        pltpu.VMEM((2, (N_NEAR // 2) * QT, cols), F32)],
        compiler_params=_cparams("parallel", "parallel", "arbitrary"),
        name="nsa_sel_win_attn",
    )(qt, selb, ksa, vst, kw, vwt, tt, oct, gates_t)


def _heads(t, h):
    b, s, _ = t.shape
    return t.reshape(b, s, h, HEAD_DIM).transpose(0, 2, 1, 3)


def _mixer_ab(x, g, scale, shift, w_in, conv_w, conv_b, dt_bias, a_log, d_skip, ssm_norm):
    b, s, d = x.shape
    o3 = 3 * SB_WIDTH
    qs = HEAD_DIM ** -0.5
    w_q, w_k, w_v = w_in[:, :SB_WIDTH] * qs, w_in[:, SB_WIDTH:2 * SB_WIDTH], w_in[:, 2 * SB_WIDTH:o3]
    w_z = w_in[:, o3:o3 + SSM_INNER]
    w_xbc = w_in[:, o3 + SSM_INNER:o3 + SSM_INNER + SSM_CONV_DIM]
    w_dt = w_in[:, o3 + SSM_INNER + SSM_CONV_DIM:]
    w_main = jnp.concatenate([w_xbc, w_z, w_q, w_k, w_v], axis=1).astype(BF16)
    w_dt = jnp.pad(w_dt, ((0, 0), (0, LANES - SSM_HEADS))).astype(BF16)
    proj, dt_raw = _norm_proj(x, g, scale, shift, w_main, w_dt, 512, "ab_in_proj")
    c0 = SSM_CONV_DIM + SSM_INNER
    o_sb = _sb_attention(proj, c0, c0 + SB_WIDTH, c0 + 2 * SB_WIDTH)
    y = _ssd(proj, dt_raw, conv_w, conv_b, dt_bias, a_log, d_skip, ssm_norm)
    return o_sb, y


def _mixer_nsa(x, g, scale, shift, tt, cmp_tiles, rel_bias, w_in, cmp_w1, cmp_b1, cmp_w2, cmp_pe):
    b, s, d = x.shape
    gq, hg = NSA_GROUPS, NSA_HG
    qs = HEAD_DIM ** -0.5
    n_main = NSA_Q + 6 * NSA_KV
    w_main = jnp.concatenate([w_in[:, :NSA_Q] * qs, w_in[:, NSA_Q:n_main]], axis=1).astype(BF16)
    n_gate = NSA_HEADS * N_BRANCH
    w_gate = jnp.pad(w_in[:, n_main:], ((0, 0), (0, LANES - n_gate))).astype(BF16)
    proj, gates = _norm_proj(x, g, scale, shift, w_main, w_gate, 512, "nsa_in_proj")
    gates_t = gates[..., :n_gate].reshape(b, s, gq, hg * N_BRANCH).transpose(0, 2, 3, 1)
    gates_t = jnp.pad(gates_t, ((0, 0), (0, 0), (0, 16 - hg * N_BRANCH), (0, 0)))
    nq = s // QT
    q5 = proj[..., :NSA_Q].reshape(b, nq, QT, gq, hg, HEAD_DIM)
    qt = q5.transpose(0, 3, 1, 5, 4, 2).reshape(b, gq, nq, HEAD_DIM, hg * QT)
    part = lambda j: proj[..., NSA_Q + j * NSA_KV:NSA_Q + (j + 1) * NSA_KV]
    nc = s // CMP_STRIDE
    kvc = jnp.stack([part(0), part(1)])
    a = kvc.reshape(2, b, nc, CMP_STRIDE, gq, HEAD_DIM).transpose(0, 1, 4, 2, 3, 5)
    a = a.reshape(2, b, gq, nc, CMP_STRIDE * HEAD_DIM)
    cmp = _compress(a, cmp_w1.astype(BF16), cmp_b1, cmp_w2.astype(BF16), cmp_pe)
    oct, selb = _cattn(rel_bias, qt, cmp[0], cmp[1].transpose(0, 1, 3, 2), cmp_tiles)
    grp = lambda t: t.reshape(b, s, gq, HEAD_DIM).transpose(0, 2, 1, 3)
    onehot = (jnp.arange(s)[:, None] // SEL_BLOCK == jnp.arange(LANES)[None, :]).astype(BF16)
    ksa = jnp.concatenate([grp(part(2)), jnp.broadcast_to(onehot, (b, gq, s, LANES))], axis=-1)

    def values_t(t):
        vt = t.reshape(b, s, gq, HEAD_DIM).transpose(0, 2, 3, 1)
        ones = jnp.ones((b, gq, 1, s), BF16)
        return jnp.concatenate([vt, ones, jnp.zeros((b, gq, VROWS - HEAD_DIM - 1, s), BF16)], axis=2)

    ot = _sattn(qt, selb, ksa, values_t(part(3)), grp(part(4)), values_t(part(5)), tt, oct, gates_t)
    o = ot.reshape(b, gq, nq, HEAD_DIM, hg, QT).transpose(0, 2, 5, 1, 4, 3)
    return o.reshape(b, s, NSA_Q)


def kernel(x, c, rel_bias, ada_w, ada_b, norm_g, ab_w_in, ab_conv_w, ab_conv_b, ab_dt_bias, ab_a_log,
           ab_d_skip, ab_ssm_norm, ab_w_out, nsa_w_in, nsa_cmp_w1, nsa_cmp_b1, nsa_cmp_w2, nsa_cmp_pe,
           nsa_w_out, ffn_w_up, ffn_conv_w, ffn_conv_b, ffn_w_down):
    depth = ada_w.shape[0]
    d = x.shape[-1]
    assert x.shape[1] % (CMP_STRIDE * QT) == 0 and x.shape[1] // SEL_BLOCK <= LANES
    mod = _modulation(c, ada_w, ada_b)
    tt = _bias_tiles(rel_bias) if depth > 1 else None
    cmp_tiles = _cmp_bias_tiles(rel_bias) if depth > 1 else None
    for l in range(depth):
        shift1, scale1, gate1, shift2, scale2, gate2 = [mod[l, :, j * d:(j + 1) * d] for j in range(6)]
        if l % 2 == 0:
            e = l // 2
            o_sb, y = _mixer_ab(x, norm_g[l, 0], scale1, shift1, ab_w_in[e], ab_conv_w[e], ab_conv_b[e],
                                ab_dt_bias[e], ab_a_log[e], ab_d_skip[e], ab_ssm_norm[e])
            w_out = ab_w_out[e].astype(BF16)
            x = _out_proj([o_sb, y], [w_out[:SB_WIDTH], w_out[SB_WIDTH:]], norm_g[l, 1], gate1, x, "ab_out_proj")
        else:
            o = l // 2
            a = _mixer_nsa(x, norm_g[l, 0], scale1, shift1, tt, cmp_tiles, rel_bias, nsa_w_in[o], nsa_cmp_w1[o],
                           nsa_cmp_b1[o], nsa_cmp_w2[o], nsa_cmp_pe[o])
            x = _out_proj([a], [nsa_w_out[o].astype(BF16)], norm_g[l, 1], gate1, x, "nsa_out_proj")
        a = _ffn_up(x, norm_g[l, 2], scale2, shift2, ffn_w_up[l].astype(BF16), ffn_conv_w[l], ffn_conv_b[l])
        x = _out_proj([a], [ffn_w_down[l].astype(BF16)], norm_g[l, 3], gate2, x, "ffn_down_proj")
    return x
```
